```python
import jax, jax.numpy as jnp
from jax import lax
import numpy as np

D_MODEL = 1024
BATCH = 1
SEQ = 16384
DEPTH = 1

W_A = 3 * D_MODEL // 2
CONV_K = 3
CHUNK = 128
DH_B = 128
H_B = D_MODEL // DH_B
W_B = H_B * DH_B
D_FF = 4 * D_MODEL
N_PROJ = 3 * W_A + 2 * W_B + 2 * D_MODEL
LN_EPS = 1e-5
ALPHA = (2.0 * DEPTH) ** 0.25
BETA = (8.0 * DEPTH) ** -0.25

OFF_BA = 0
OFF_CA = OFF_BA + W_A
OFF_HA = OFF_CA + W_A
OFF_UB = OFF_HA + W_A
OFF_VB = OFF_UB + W_B
OFF_GA = OFF_VB + W_B
OFF_GB = OFF_GA + D_MODEL

kernel_name = "hybrid_shortconv_gmlp_deepnorm_encoder"


def _layernorm(x, g, b):
    xf = x.astype(jnp.float32)
    mu = jnp.mean(xf, axis=-1, keepdims=True)
    var = jnp.mean(jnp.square(xf - mu), axis=-1, keepdims=True)
    y = (xf - mu) * lax.rsqrt(var + LN_EPS) * g.astype(jnp.float32) + b.astype(jnp.float32)
    return y.astype(x.dtype)


def _centred_depthwise_conv(h, w):
    hp = jnp.pad(h, ((0, 0), (1, 1), (0, 0)))
    return hp[:, :-2, :] * w[0] + hp[:, 1:-1, :] * w[1] + hp[:, 2:, :] * w[2]


def _spatial_gate(u, v, g, b, w_s, b_s):
    bsz = v.shape[0]
    s = v.shape[1]
    n_chunks = s // CHUNK
    v = _layernorm(v, g, b)
    vc = v.reshape(bsz, n_chunks, CHUNK, H_B, DH_B)
    mixed = jnp.einsum('hij,bcjhd->bcihd', w_s, vc)
    mixed = mixed + jnp.transpose(b_s)[None, None, :, :, None]
    return u * mixed.reshape(bsz, s, W_B)


def setup_inputs(seed: int = 0) -> dict:
    key = jax.random.key(seed)
    ks = jax.random.split(key, 20)
    L = DEPTH

    def nrm(k, shape, scale):
        return jax.random.normal(k, shape, jnp.float32) * scale

    return {
        "x": nrm(ks[0], (BATCH, SEQ, D_MODEL), 1.0),
        "w_in": nrm(ks[1], (L, D_MODEL, N_PROJ), D_MODEL ** -0.5),
        "b_gate": nrm(ks[2], (L, 2 * D_MODEL), 0.02),
        "conv_w": nrm(ks[3], (L, CONV_K, W_A), CONV_K ** -0.5),
        "v_norm_g": 1.0 + nrm(ks[4], (L, W_B), 0.02),
        "v_norm_b": nrm(ks[5], (L, W_B), 0.02),
        "w_s": nrm(ks[6], (L, H_B, CHUNK, CHUNK), CHUNK ** -0.5),
        "b_s": 1.0 + nrm(ks[7], (L, H_B, CHUNK), 0.02),
        "w_pa": nrm(ks[8], (L, W_A, D_MODEL), W_A ** -0.5),
        "w_pb": nrm(ks[9], (L, W_B, D_MODEL), W_B ** -0.5),
        "w_o": nrm(ks[10], (L, D_MODEL, D_MODEL), BETA * D_MODEL ** -0.5),
        "ln1_g": 1.0 + nrm(ks[11], (L, D_MODEL), 0.02),
        "ln1_b": nrm(ks[12], (L, D_MODEL), 0.02),
        "w_ff1": nrm(ks[13], (L, D_MODEL, D_FF), BETA * D_MODEL ** -0.5),
        "w_ff2": nrm(ks[14], (L, D_FF, D_MODEL), BETA * D_FF ** -0.5),
        "ln2_g": 1.0 + nrm(ks[15], (L, D_MODEL), 0.02),
        "ln2_b": nrm(ks[16], (L, D_MODEL), 0.02),
    }


def reference(x, w_in, b_gate, conv_w, v_norm_g, v_norm_b, w_s, b_s, w_pa, w_pb, w_o,
              ln1_g, ln1_b, w_ff1, w_ff2, ln2_g, ln2_b):
    for l in range(DEPTH):
        p = jnp.einsum('bsd,dn->bsn', x, w_in[l])
        b_a = p[:, :, OFF_BA:OFF_CA]
        c_a = p[:, :, OFF_CA:OFF_HA]
        h_a = p[:, :, OFF_HA:OFF_UB]
        u_b = p[:, :, OFF_UB:OFF_VB]
        v_b = p[:, :, OFF_VB:OFF_GA]
        gates = jax.nn.sigmoid(p[:, :, OFF_GA:N_PROJ] + b_gate[l])
        g_a = gates[:, :, :D_MODEL]
        g_b = gates[:, :, D_MODEL:]
        a = b_a * _centred_depthwise_conv(c_a * h_a, conv_w[l])
        bb = _spatial_gate(jax.nn.gelu(u_b), jax.nn.gelu(v_b), v_norm_g[l], v_norm_b[l],
                           w_s[l], b_s[l])
        y_a = jnp.einsum('bsc,cd->bsd', a, w_pa[l])
        y_b = jnp.einsum('bsc,cd->bsd', bb, w_pb[l])
        mix = jnp.einsum('bsd,de->bse', g_a * y_a + g_b * y_b, w_o[l])
        x = _layernorm(ALPHA * x + mix, ln1_g[l], ln1_b[l])
        hid = jnp.square(jax.nn.relu(jnp.einsum('bsd,df->bsf', x, w_ff1[l])))
        ffn = jnp.einsum('bsf,fd->bsd', hid, w_ff2[l])
        x = _layernorm(ALPHA * x + ffn, ln2_g[l], ln2_b[l])
    return x
```

```python
import functools

import jax
import jax.numpy as jnp
from jax import lax
from jax.experimental import pallas as pl
from jax.experimental.pallas import tpu as pltpu

CONV_K = 3
CHUNK = 128
DH_B = 128
LN_EPS = 1e-5

HALO = 16
VMEM_LIMIT_BYTES = 60 * 1024 * 1024

SEQ_TILE_MIX = 512
SEQ_TILE_FFN = 512
FF_CHUNK = 1024


def _layernorm(y, g, b):
    mu = jnp.mean(y, axis=-1, keepdims=True)
    yc = y - mu
    var = jnp.mean(yc * yc, axis=-1, keepdims=True)
    return yc * lax.rsqrt(var + LN_EPS) * g + b


def _sigmoid(z):
    return 0.5 * jnp.tanh(0.5 * z) + 0.5


def _dot(a, b):
    return jnp.dot(a, b, preferred_element_type=jnp.float32)


def _mix_kernel(xprev_ref, x_ref, xnext_ref, w_in_ref, bgate_ref, convw_ref,
                vg_ref, vb_ref, ws_ref, bs_ref, wpa_ref, wpb_ref, wo_ref,
                g1_ref, b1_ref, o_ref, xe_ref, ch_ref, *, alpha, w_a, w_b, d_model):
    ts = x_ref.shape[0]
    i = pl.program_id(0)
    n = pl.num_programs(0)
    off_ca = w_a
    off_ha = 2 * w_a
    off_ub = 3 * w_a
    off_vb = off_ub + w_b
    off_ga = off_vb + w_b
    off_gb = off_ga + d_model

    x = x_ref[...]
    xe_ref[pl.ds(0, HALO), :] = jnp.where(i > 0, xprev_ref[...], 0.0).astype(jnp.bfloat16)
    xe_ref[pl.ds(HALO, ts), :] = x.astype(jnp.bfloat16)
    xe_ref[pl.ds(HALO + ts, HALO), :] = jnp.where(i < n - 1, xnext_ref[...], 0.0).astype(jnp.bfloat16)

    xe = xe_ref[...]
    ch_ref[...] = _dot(xe, w_in_ref[:, off_ca:off_ha]) * _dot(xe, w_in_ref[:, off_ha:off_ub])
    xm = xe_ref[pl.ds(HALO, ts), :]
    cw = convw_ref[...]
    conv = (ch_ref[pl.ds(HALO - 1, ts), :] * cw[0:1, :]
            + ch_ref[pl.ds(HALO, ts), :] * cw[1:2, :]
            + ch_ref[pl.ds(HALO + 1, ts), :] * cw[2:3, :])
    a = _dot(xm, w_in_ref[:, 0:off_ca]) * conv
    y_a = _dot(a.astype(jnp.bfloat16), wpa_ref[...])
    g_a = _sigmoid(_dot(xm, w_in_ref[:, off_ga:off_gb]) + bgate_ref[:, 0:d_model])
    acc = g_a * y_a

    v = jax.nn.gelu(_dot(xm, w_in_ref[:, off_vb:off_ga]))
    v = _layernorm(v, vg_ref[...], vb_ref[...]).astype(jnp.bfloat16)
    n_chunks = ts // CHUNK
    n_heads = w_b // DH_B
    head_cols = []
    for h in range(n_heads):
        vh = jnp.concatenate(
            [v[c * CHUNK:(c + 1) * CHUNK, h * DH_B:(h + 1) * DH_B] for c in range(n_chunks)],
            axis=1)
        mh = _dot(ws_ref[h], vh)
        mh = mh + jnp.concatenate([bs_ref[:, h * DH_B:(h + 1) * DH_B]] * n_chunks, axis=1)
        head_cols.append(jnp.concatenate(
            [mh[:, c * DH_B:(c + 1) * DH_B] for c in range(n_chunks)], axis=0))
    mixed = jnp.concatenate(head_cols, axis=1)
    u = jax.nn.gelu(_dot(xm, w_in_ref[:, off_ub:off_vb]))
    y_b = _dot((u * mixed).astype(jnp.bfloat16), wpb_ref[...])
    g_b = _sigmoid(_dot(xm, w_in_ref[:, off_gb:off_gb + d_model]) + bgate_ref[:, d_model:2 * d_model])
    acc = acc + g_b * y_b

    mix = _dot(acc.astype(jnp.bfloat16), wo_ref[...])
    o_ref[...] = _layernorm(alpha * x + mix, g1_ref[...], b1_ref[...])


def _ffn_kernel(x_ref, w1_ref, w2_ref, g2_ref, b2_ref, o_ref, *, alpha):
    x = x_ref[...]
    xb = x.astype(jnp.bfloat16)
    d_ff = w1_ref.shape[1]
    ffn = None
    for k in range(d_ff // FF_CHUNK):
        hid = jnp.maximum(_dot(xb, w1_ref[:, k * FF_CHUNK:(k + 1) * FF_CHUNK]), 0.0)
        part = _dot((hid * hid).astype(jnp.bfloat16), w2_ref[k * FF_CHUNK:(k + 1) * FF_CHUNK, :])
        ffn = part if ffn is None else ffn + part
    o_ref[...] = _layernorm(alpha * x + ffn, g2_ref[...], b2_ref[...])


def _resident(shape):
    nd = len(shape)
    return pl.BlockSpec(shape, lambda i: (0,) * nd, pipeline_mode=pl.Buffered(1))


def _mix_layer(x2d, w_in, b_gate, conv_w, vg, vb, w_s, bs_full, w_pa, w_pb, w_o, g1, b1, *, alpha):
    s, d = x2d.shape
    w_a = w_pa.shape[0]
    w_b = w_pb.shape[0]
    ts = SEQ_TILE_MIX
    assert s % ts == 0 and ts % CHUNK == 0 and ts % HALO == 0
    hb = ts // HALO
    n_halo_blocks = s // HALO
    kern = functools.partial(_mix_kernel, alpha=alpha, w_a=w_a, w_b=w_b, d_model=d)
    return pl.pallas_call(
        kern,
        grid=(s // ts,),
        in_specs=[
            pl.BlockSpec((HALO, d), lambda i: (jnp.maximum(i * hb - 1, 0), 0)),
            pl.BlockSpec((ts, d), lambda i: (i, 0)),
            pl.BlockSpec((HALO, d), lambda i: (jnp.minimum((i + 1) * hb, n_halo_blocks - 1), 0)),
            _resident(w_in.shape), _resident(b_gate.shape), _resident(conv_w.shape),
            _resident(vg.shape), _resident(vb.shape), _resident(w_s.shape),
            _resident(bs_full.shape), _resident(w_pa.shape), _resident(w_pb.shape),
            _resident(w_o.shape), _resident(g1.shape), _resident(b1.shape),
        ],
        out_specs=pl.BlockSpec((ts, d), lambda i: (i, 0)),
        out_shape=jax.ShapeDtypeStruct((s, d), jnp.float32),
        scratch_shapes=[
            pltpu.VMEM((ts + 2 * HALO, d), jnp.bfloat16),
            pltpu.VMEM((ts + 2 * HALO, w_a), jnp.float32),
        ],
        compiler_params=pltpu.CompilerParams(
            dimension_semantics=("arbitrary",), vmem_limit_bytes=VMEM_LIMIT_BYTES),
        name="token_mix",
    )(x2d, x2d, x2d, w_in, b_gate, conv_w, vg, vb, w_s, bs_full, w_pa, w_pb, w_o, g1, b1)


def _ffn_layer(x2d, w1, w2, g2, b2, *, alpha):
    s, d = x2d.shape
    ts = SEQ_TILE_FFN
    assert s % ts == 0 and w1.shape[1] % FF_CHUNK == 0
    return pl.pallas_call(
        functools.partial(_ffn_kernel, alpha=alpha),
        grid=(s // ts,),
        in_specs=[
            pl.BlockSpec((ts, d), lambda i: (i, 0)),
            _resident(w1.shape), _resident(w2.shape), _resident(g2.shape), _resident(b2.shape),
        ],
        out_specs=pl.BlockSpec((ts, d), lambda i: (i, 0)),
        out_shape=jax.ShapeDtypeStruct((s, d), jnp.float32),
        compiler_params=pltpu.CompilerParams(
            dimension_semantics=("arbitrary",), vmem_limit_bytes=VMEM_LIMIT_BYTES),
        name="channel_mix",
    )(x2d, w1, w2, g2, b2)


def kernel(x, w_in, b_gate, conv_w, v_norm_g, v_norm_b, w_s, b_s, w_pa, w_pb, w_o,
           ln1_g, ln1_b, w_ff1, w_ff2, ln2_g, ln2_b):
    bsz, seq, d = x.shape
    depth = w_in.shape[0]
    alpha = (2.0 * depth) ** 0.25
    bf = jnp.bfloat16
    row = lambda p: p.reshape(1, -1)
    outs = []
    for b in range(bsz):
        xb = x[b]
        for l in range(depth):
            bs_full = jnp.repeat(jnp.transpose(b_s[l]), DH_B, axis=1)
            xb = _mix_layer(
                xb, w_in[l].astype(bf), row(b_gate[l]), conv_w[l], row(v_norm_g[l]),
                row(v_norm_b[l]), w_s[l].astype(bf), bs_full, w_pa[l].astype(bf),
                w_pb[l].astype(bf), w_o[l].astype(bf), row(ln1_g[l]), row(ln1_b[l]), alpha=alpha)
            xb = _ffn_layer(xb, w_ff1[l].astype(bf), w_ff2[l].astype(bf),
                            row(ln2_g[l]), row(ln2_b[l]), alpha=alpha)
        outs.append(xb)
    return jnp.stack(outs, axis=0) if bsz > 1 else outs[0][None]
```

```python
import functools

import jax
import jax.numpy as jnp
from jax import lax
from jax.experimental import pallas as pl
from jax.experimental.pallas import tpu as pltpu

CONV_K = 3
CHUNK = 128
DH_B = 128
LN_EPS = 1e-5

HALO = 16
VMEM_LIMIT_BYTES = 60 * 1024 * 1024

SEQ_TILE_MIX = 512
SEQ_TILE_FFN = 512
FF_CHUNK = 1024


def _layernorm(y, g, b):
    mu = jnp.mean(y, axis=-1, keepdims=True)
    yc = y - mu
    var = jnp.mean(yc * yc, axis=-1, keepdims=True)
    return yc * lax.rsqrt(var + LN_EPS) * g + b


def _sigmoid(z):
    return 0.5 * jnp.tanh(0.5 * z) + 0.5


def _dot(a, b):
    return jnp.dot(a, b, preferred_element_type=jnp.float32)


def _deferred_norm_steps(carry_ref, o_ref, g_ref, b_ref, tile_body):
    i = pl.program_id(0)
    n_tiles = pl.num_programs(0) - 1

    @pl.when(i == 0)
    def _():
        carry_ref[...] = jnp.zeros_like(carry_ref)

    @pl.when(i < n_tiles)
    def _():
        o_ref[...] = _layernorm(carry_ref[...], g_ref[...], b_ref[...])
        tile_body()

    @pl.when(i == n_tiles)
    def _():
        o_ref[...] = _layernorm(carry_ref[...], g_ref[...], b_ref[...])


def _mix_kernel(xprev_ref, x_ref, xnext_ref, w_in_ref, bgate_ref, convw_ref,
                vg_ref, vb_ref, ws_ref, bs_ref, wpa_ref, wpb_ref, wo_ref,
                g1_ref, b1_ref, w1f_ref, w2f_ref, o_ref, w1b_ref, w2b_ref,
                xe_ref, ch_ref, carry_ref, *, alpha, w_a, w_b, d_model):
    ts = x_ref.shape[0]
    i = pl.program_id(0)
    n_tiles = pl.num_programs(0) - 1
    off_ca = w_a
    off_ha = 2 * w_a
    off_ub = 3 * w_a
    off_vb = off_ub + w_b
    off_ga = off_vb + w_b
    off_gb = off_ga + d_model

    def tile_body():
        w1b_ref[...] = w1f_ref[...].astype(jnp.bfloat16)
        w2b_ref[...] = w2f_ref[...].astype(jnp.bfloat16)

        x = x_ref[...]
        xe_ref[pl.ds(0, HALO), :] = jnp.where(i > 0, xprev_ref[...], 0.0).astype(jnp.bfloat16)
        xe_ref[pl.ds(HALO, ts), :] = x.astype(jnp.bfloat16)
        xe_ref[pl.ds(HALO + ts, HALO), :] = jnp.where(
            i < n_tiles - 1, xnext_ref[...], 0.0).astype(jnp.bfloat16)

        xe = xe_ref[...]
        ch_ref[...] = _dot(xe, w_in_ref[:, off_ca:off_ha]) * _dot(xe, w_in_ref[:, off_ha:off_ub])
        xm = xe_ref[pl.ds(HALO, ts), :]
        cw = convw_ref[...]
        conv = (ch_ref[pl.ds(HALO - 1, ts), :] * cw[0:1, :]
                + ch_ref[pl.ds(HALO, ts), :] * cw[1:2, :]
                + ch_ref[pl.ds(HALO + 1, ts), :] * cw[2:3, :])
        a = _dot(xm, w_in_ref[:, 0:off_ca]) * conv
        y_a = _dot(a.astype(jnp.bfloat16), wpa_ref[...])
        g_a = _sigmoid(_dot(xm, w_in_ref[:, off_ga:off_gb]) + bgate_ref[:, 0:d_model])
        carry_ref[...] = g_a * y_a

        v = jax.nn.gelu(_dot(xm, w_in_ref[:, off_vb:off_ga]))
        v = _layernorm(v, vg_ref[...], vb_ref[...]).astype(jnp.bfloat16)
        n_chunks = ts // CHUNK
        n_heads = w_b // DH_B
        head_cols = []
        for h in range(n_heads):
            vh = jnp.concatenate(
                [v[c * CHUNK:(c + 1) * CHUNK, h * DH_B:(h + 1) * DH_B] for c in range(n_chunks)],
                axis=1)
            mh = _dot(ws_ref[h], vh)
            mh = mh + jnp.concatenate([bs_ref[:, h * DH_B:(h + 1) * DH_B]] * n_chunks, axis=1)
            head_cols.append(jnp.concatenate(
                [mh[:, c * DH_B:(c + 1) * DH_B] for c in range(n_chunks)], axis=0))
        mixed = jnp.concatenate(head_cols, axis=1)
        u = jax.nn.gelu(_dot(xm, w_in_ref[:, off_ub:off_vb]))
        y_b = _dot((u * mixed).astype(jnp.bfloat16), wpb_ref[...])
        g_b = _sigmoid(_dot(xm, w_in_ref[:, off_gb:off_gb + d_model])
                       + bgate_ref[:, d_model:2 * d_model])
        acc = carry_ref[...] + g_b * y_b

        carry_ref[...] = alpha * x + _dot(acc.astype(jnp.bfloat16), wo_ref[...])

    _deferred_norm_steps(carry_ref, o_ref, g1_ref, b1_ref, tile_body)


def _ffn_kernel(x_ref, w1_ref, w2_ref, g2_ref, b2_ref, o_ref, carry_ref, *, alpha):
    def tile_body():
        x = x_ref[...]
        xb = x.astype(jnp.bfloat16)
        for k in range(w1_ref.shape[1] // FF_CHUNK):
            hid = jnp.maximum(_dot(xb, w1_ref[:, k * FF_CHUNK:(k + 1) * FF_CHUNK]), 0.0)
            part = _dot((hid * hid).astype(jnp.bfloat16),
                        w2_ref[k * FF_CHUNK:(k + 1) * FF_CHUNK, :])
            carry_ref[...] = (alpha * x if k == 0 else carry_ref[...]) + part

    _deferred_norm_steps(carry_ref, o_ref, g2_ref, b2_ref, tile_body)


def _resident(shape):
    nd = len(shape)
    return pl.BlockSpec(shape, lambda i: (0,) * nd, pipeline_mode=pl.Buffered(1))


def _mix_layer(x2d, w_in, b_gate, conv_w, vg, vb, w_s, bs_full, w_pa, w_pb, w_o, g1, b1,
               w_ff1, w_ff2, *, alpha):
    s, d = x2d.shape
    w_a = w_pa.shape[0]
    w_b = w_pb.shape[0]
    d_ff = w_ff1.shape[1]
    ts = SEQ_TILE_MIX
    assert s % ts == 0 and ts % CHUNK == 0 and ts % HALO == 0
    n_tiles = s // ts
    hb = ts // HALO
    n_halo_blocks = s // HALO
    r1, r2 = d // n_tiles, d_ff // n_tiles
    assert r1 * n_tiles == d and r2 * n_tiles == d_ff and r1 % HALO == 0 and r2 % HALO == 0
    last = n_tiles - 1
    tile = lambda i: jnp.minimum(i, last)
    kern = functools.partial(_mix_kernel, alpha=alpha, w_a=w_a, w_b=w_b, d_model=d)
    return pl.pallas_call(
        kern,
        grid=(n_tiles + 1,),
        in_specs=[
            pl.BlockSpec((HALO, d), lambda i: (jnp.maximum(tile(i) * hb - 1, 0), 0)),
            pl.BlockSpec((ts, d), lambda i: (tile(i), 0)),
            pl.BlockSpec((HALO, d),
                         lambda i: (jnp.minimum((tile(i) + 1) * hb, n_halo_blocks - 1), 0)),
            _resident(w_in.shape), _resident(b_gate.shape), _resident(conv_w.shape),
            _resident(vg.shape), _resident(vb.shape), _resident(w_s.shape),
            _resident(bs_full.shape), _resident(w_pa.shape), _resident(w_pb.shape),
            _resident(w_o.shape), _resident(g1.shape), _resident(b1.shape),
            pl.BlockSpec((r1, d_ff), lambda i: (tile(i), 0)),
            pl.BlockSpec((r2, d), lambda i: (tile(i), 0)),
        ],
        out_specs=[
            pl.BlockSpec((ts, d), lambda i: (jnp.maximum(i - 1, 0), 0)),
            pl.BlockSpec((r1, d_ff), lambda i: (tile(i), 0)),
            pl.BlockSpec((r2, d), lambda i: (tile(i), 0)),
        ],
        out_shape=[
            jax.ShapeDtypeStruct((s, d), jnp.float32),
            jax.ShapeDtypeStruct(w_ff1.shape, jnp.bfloat16),
            jax.ShapeDtypeStruct(w_ff2.shape, jnp.bfloat16),
        ],
        scratch_shapes=[
            pltpu.VMEM((ts + 2 * HALO, d), jnp.bfloat16),
            pltpu.VMEM((ts + 2 * HALO, w_a), jnp.float32),
            pltpu.VMEM((ts, d), jnp.float32),
        ],
        compiler_params=pltpu.CompilerParams(
            dimension_semantics=("arbitrary",), vmem_limit_bytes=VMEM_LIMIT_BYTES),
        name="token_mix",
    )(x2d, x2d, x2d, w_in, b_gate, conv_w, vg, vb, w_s, bs_full, w_pa, w_pb, w_o, g1, b1,
      w_ff1, w_ff2)


def _ffn_layer(x2d, w1, w2, g2, b2, *, alpha):
    s, d = x2d.shape
    ts = SEQ_TILE_FFN
    assert s % ts == 0 and w1.shape[1] % FF_CHUNK == 0
    n_tiles = s // ts
    return pl.pallas_call(
        functools.partial(_ffn_kernel, alpha=alpha),
        grid=(n_tiles + 1,),
        in_specs=[
            pl.BlockSpec((ts, d), lambda i: (jnp.minimum(i, n_tiles - 1), 0)),
            _resident(w1.shape), _resident(w2.shape), _resident(g2.shape), _resident(b2.shape),
        ],
        out_specs=pl.BlockSpec((ts, d), lambda i: (jnp.maximum(i - 1, 0), 0)),
        out_shape=jax.ShapeDtypeStruct((s, d), jnp.float32),
        scratch_shapes=[pltpu.VMEM((ts, d), jnp.float32)],
        compiler_params=pltpu.CompilerParams(
            dimension_semantics=("arbitrary",), vmem_limit_bytes=VMEM_LIMIT_BYTES),
        name="channel_mix",
    )(x2d, w1, w2, g2, b2)


def kernel(x, w_in, b_gate, conv_w, v_norm_g, v_norm_b, w_s, b_s, w_pa, w_pb, w_o,
           ln1_g, ln1_b, w_ff1, w_ff2, ln2_g, ln2_b):
    bsz, seq, d = x.shape
    depth = w_in.shape[0]
    alpha = (2.0 * depth) ** 0.25
    bf = jnp.bfloat16
    row = lambda p: p.reshape(1, -1)
    outs = []
    for b in range(bsz):
        xb = x[b]
        for l in range(depth):
            bs_full = jnp.repeat(jnp.transpose(b_s[l]), DH_B, axis=1)
            xb, w1b, w2b = _mix_layer(
                xb, w_in[l].astype(bf), row(b_gate[l]), conv_w[l], row(v_norm_g[l]),
                row(v_norm_b[l]), w_s[l].astype(bf), bs_full, w_pa[l].astype(bf),
                w_pb[l].astype(bf), w_o[l].astype(bf), row(ln1_g[l]), row(ln1_b[l]),
                w_ff1[l], w_ff2[l], alpha=alpha)
            xb = _ffn_layer(xb, w1b, w2b, row(ln2_g[l]), row(ln2_b[l]), alpha=alpha)
        outs.append(xb)
    return jnp.stack(outs, axis=0) if bsz > 1 else outs[0][None]
```

```python
import functools

import jax
import jax.numpy as jnp
from jax import lax
from jax.experimental import pallas as pl
from jax.experimental.pallas import tpu as pltpu

CONV_K = 3
CHUNK = 128
DH_B = 128
LN_EPS = 1e-5

LANES = 128
BF16_ROWS = 16
HALO = BF16_ROWS
VMEM_LIMIT_BYTES = 60 * 1024 * 1024

SEQ_TILE_MIX = 512
SEQ_TILE_FFN = 512
FF_CHUNK = 1024
N_NORM_BLOCKS = 4


def _layernorm(y, g, b):
    mu = jnp.mean(y, axis=-1, keepdims=True)
    yc = y - mu
    var = jnp.mean(yc * yc, axis=-1, keepdims=True)
    return yc * lax.rsqrt(var + LN_EPS) * g + b


def _sigmoid(z):
    return 0.5 * jnp.tanh(0.5 * z) + 0.5


def _dot(a, b):
    return jnp.dot(a, b, preferred_element_type=jnp.float32)


def _norm_carried_tile(carry_ref, o_ref, g_ref, b_ref):
    rows = carry_ref.shape[0] // N_NORM_BLOCKS
    d = carry_ref.shape[1]
    tokens = []
    for r in range(N_NORM_BLOCKS):
        y = _layernorm(carry_ref[pl.ds(r * rows, rows), :], g_ref[...], b_ref[...])
        o_ref[pl.ds(r * rows, rows), :] = y
        t = y.reshape(rows // BF16_ROWS, BF16_ROWS, d).sum(axis=0)
        t = functools.reduce(
            lambda p, q: p + q, [t[:, c * LANES:(c + 1) * LANES] for c in range(d // LANES)])
        tokens.append(t.astype(jnp.bfloat16))
    return tokens


def _tie(value, token):
    never = pl.program_id(0) < 0
    r, c = token.shape
    head = jnp.where(never, token, value[0:r, 0:c])
    top = jnp.concatenate([head, value[0:r, c:]], axis=1)
    return jnp.concatenate([top, value[r:]], axis=0)


def _deferred_norm_steps(carry_ref, o_ref, g_ref, b_ref, tile_body):
    i = pl.program_id(0)
    n_tiles = pl.num_programs(0) - 1

    @pl.when(i == 0)
    def _():
        carry_ref[...] = jnp.zeros_like(carry_ref)

    @pl.when(i < n_tiles)
    def _():
        tile_body(_norm_carried_tile(carry_ref, o_ref, g_ref, b_ref))

    @pl.when(i == n_tiles)
    def _():
        _norm_carried_tile(carry_ref, o_ref, g_ref, b_ref)


def _mix_kernel(xprev_ref, x_ref, xnext_ref, w_in_ref, bgate_ref, convw_ref,
                vg_ref, vb_ref, ws_ref, bs_ref, wpa_ref, wpb_ref, wo_ref,
                g1_ref, b1_ref, w1f_ref, w2f_ref, o_ref, w1b_ref, w2b_ref,
                xe_ref, ch_ref, carry_ref, *, alpha, w_a, w_b, d_model):
    ts = x_ref.shape[0]
    i = pl.program_id(0)
    n_tiles = pl.num_programs(0) - 1
    off_ca = w_a
    off_ha = 2 * w_a
    off_ub = 3 * w_a
    off_vb = off_ub + w_b
    off_ga = off_vb + w_b
    off_gb = off_ga + d_model

    def tile_body(tokens):
        w1b_ref[...] = w1f_ref[...].astype(jnp.bfloat16)
        w2b_ref[...] = w2f_ref[...].astype(jnp.bfloat16)

        x = x_ref[...]
        xe_ref[pl.ds(0, HALO), :] = jnp.where(i > 0, xprev_ref[...], 0.0).astype(jnp.bfloat16)
        xe_ref[pl.ds(HALO, ts), :] = x.astype(jnp.bfloat16)
        xe_ref[pl.ds(HALO + ts, HALO), :] = jnp.where(
            i < n_tiles - 1, xnext_ref[...], 0.0).astype(jnp.bfloat16)

        xe = xe_ref[...]
        xm = xe_ref[pl.ds(HALO, ts), :]
        n_chunks = ts // CHUNK
        n_heads = w_b // DH_B

        half = w_a // 2
        c_lo = _dot(xe, w_in_ref[:, off_ca:off_ca + half])
        ch_ref[:, 0:half] = c_lo * _dot(_tie(xe, tokens[0]), w_in_ref[:, off_ha:off_ha + half])
        c_hi = _dot(_tie(xe, tokens[1]), w_in_ref[:, off_ca + half:off_ha])
        ch_ref[:, half:w_a] = c_hi * _dot(_tie(xe, tokens[2]), w_in_ref[:, off_ha + half:off_ub])
        v = jax.nn.gelu(_dot(_tie(xm, tokens[3]), w_in_ref[:, off_vb:off_ga]))
        v = _layernorm(v, vg_ref[...], vb_ref[...]).astype(jnp.bfloat16)
        u = jax.nn.gelu(_dot(xm, w_in_ref[:, off_ub:off_vb]))
        b_a = _dot(xm, w_in_ref[:, 0:off_ca])

        head_cols = []
        for h in range(n_heads):
            vh = jnp.concatenate(
                [v[k * CHUNK:(k + 1) * CHUNK, h * DH_B:(h + 1) * DH_B] for k in range(n_chunks)],
                axis=1)
            mh = _dot(ws_ref[h], vh)
            mh = mh + jnp.concatenate([bs_ref[:, h * DH_B:(h + 1) * DH_B]] * n_chunks, axis=1)
            head_cols.append(jnp.concatenate(
                [mh[:, k * DH_B:(k + 1) * DH_B] for k in range(n_chunks)], axis=0))
        mixed = jnp.concatenate(head_cols, axis=1)

        cw = convw_ref[...]
        conv = (ch_ref[pl.ds(HALO - 1, ts), :] * cw[0:1, :]
                + ch_ref[pl.ds(HALO, ts), :] * cw[1:2, :]
                + ch_ref[pl.ds(HALO + 1, ts), :] * cw[2:3, :])
        a = (b_a * conv).astype(jnp.bfloat16)
        g_b = _sigmoid(_dot(xm, w_in_ref[:, off_gb:off_gb + d_model])
                       + bgate_ref[:, d_model:2 * d_model])
        y_b =_dot((u * mixed).astype(jnp.bfloat16), wpb_ref[...])
        g_a = _sigmoid(_dot(xm, w_in_ref[:, off_ga:off_gb]) + bgate_ref[:, 0:d_model])
        acc = g_b * y_b
        y_a = _dot(a, wpa_ref[...])
        acc = acc + g_a * y_a
        carry_ref[...] = alpha * x + _dot(acc.astype(jnp.bfloat16), wo_ref[...])

    _deferred_norm_steps(carry_ref, o_ref, g1_ref, b1_ref, tile_body)


def _ffn_kernel(x_ref, w1_ref, w2_ref, g2_ref, b2_ref, o_ref, carry_ref, *, alpha):
    assert w1_ref.shape[1] == N_NORM_BLOCKS * FF_CHUNK

    def tile_body(tokens):
        x = x_ref[...]
        xb = x.astype(jnp.bfloat16)
        y = alpha * x
        for k in range(N_NORM_BLOCKS):
            hid = jnp.maximum(_dot(xb, w1_ref[:, k * FF_CHUNK:(k + 1) * FF_CHUNK]), 0.0)
            hid = _tie((hid * hid).astype(jnp.bfloat16), tokens[k])
            y = y + _dot(hid, w2_ref[k * FF_CHUNK:(k + 1) * FF_CHUNK, :])
        carry_ref[...] = y

    _deferred_norm_steps(carry_ref, o_ref, g2_ref, b2_ref, tile_body)


def _resident(shape):
    nd = len(shape)
    return pl.BlockSpec(shape, lambda i: (0,) * nd, pipeline_mode=pl.Buffered(1))


def _mix_layer(x2d, w_in, b_gate, conv_w, vg, vb, w_s, bs_full, w_pa, w_pb, w_o, g1, b1,
               w_ff1, w_ff2, *, alpha):
    s, d = x2d.shape
    w_a = w_pa.shape[0]
    w_b = w_pb.shape[0]
    d_ff = w_ff1.shape[1]
    ts = SEQ_TILE_MIX
    assert s % ts == 0 and ts % CHUNK == 0 and ts % (N_NORM_BLOCKS * BF16_ROWS) == 0
    n_tiles = s // ts
    hb = ts // HALO
    n_halo_blocks = s // HALO
    r1, r2 = d // n_tiles, d_ff // n_tiles
    assert r1 * n_tiles == d and r2 * n_tiles == d_ff
    assert r1 % BF16_ROWS == 0 and r2 % BF16_ROWS == 0
    last = n_tiles - 1
    tile = lambda i: jnp.minimum(i, last)
    kern = functools.partial(_mix_kernel, alpha=alpha, w_a=w_a, w_b=w_b, d_model=d)
    return pl.pallas_call(
        kern,
        grid=(n_tiles + 1,),
        in_specs=[
            pl.BlockSpec((HALO, d), lambda i: (jnp.maximum(tile(i) * hb - 1, 0), 0)),
            pl.BlockSpec((ts, d), lambda i: (tile(i), 0)),
            pl.BlockSpec((HALO, d),
                         lambda i: (jnp.minimum((tile(i) + 1) * hb, n_halo_blocks - 1), 0)),
            _resident(w_in.shape), _resident(b_gate.shape), _resident(conv_w.shape),
            _resident(vg.shape), _resident(vb.shape), _resident(w_s.shape),
            _resident(bs_full.shape), _resident(w_pa.shape), _resident(w_pb.shape),
            _resident(w_o.shape), _resident(g1.shape), _resident(b1.shape),
            pl.BlockSpec((r1, d_ff), lambda i: (tile(i), 0)),
            pl.BlockSpec((r2, d), lambda i: (tile(i), 0)),
        ],
        out_specs=[
            pl.BlockSpec((ts, d), lambda i: (jnp.maximum(i - 1, 0), 0)),
            pl.BlockSpec((r1, d_ff), lambda i: (tile(i), 0)),
            pl.BlockSpec((r2, d), lambda i: (tile(i), 0)),
        ],
        out_shape=[
            jax.ShapeDtypeStruct((s, d), jnp.float32),
            jax.ShapeDtypeStruct(w_ff1.shape, jnp.bfloat16),
            jax.ShapeDtypeStruct(w_ff2.shape, jnp.bfloat16),
        ],
        scratch_shapes=[
            pltpu.VMEM((ts + 2 * HALO, d), jnp.bfloat16),
            pltpu.VMEM((ts + 2 * HALO, w_a), jnp.float32),
            pltpu.VMEM((ts, d), jnp.float32),
        ],
        compiler_params=pltpu.CompilerParams(
            dimension_semantics=("arbitrary",), vmem_limit_bytes=VMEM_LIMIT_BYTES),
        name="token_mix",
    )(x2d, x2d, x2d, w_in, b_gate, conv_w, vg, vb, w_s, bs_full, w_pa, w_pb, w_o, g1, b1,
      w_ff1, w_ff2)


def _ffn_layer(x2d, w1, w2, g2, b2, *, alpha):
    s, d = x2d.shape
    ts = SEQ_TILE_FFN
    assert s % ts == 0 and ts % (N_NORM_BLOCKS * BF16_ROWS) == 0
    n_tiles = s // ts
    return pl.pallas_call(
        functools.partial(_ffn_kernel, alpha=alpha),
        grid=(n_tiles + 1,),
        in_specs=[
            pl.BlockSpec((ts, d), lambda i: (jnp.minimum(i, n_tiles - 1), 0)),
            _resident(w1.shape), _resident(w2.shape), _resident(g2.shape), _resident(b2.shape),
        ],
        out_specs=pl.BlockSpec((ts, d), lambda i: (jnp.maximum(i - 1, 0), 0)),
        out_shape=jax.ShapeDtypeStruct((s, d), jnp.float32),
        scratch_shapes=[pltpu.VMEM((ts, d), jnp.float32)],
        compiler_params=pltpu.CompilerParams(
            dimension_semantics=("arbitrary",), vmem_limit_bytes=VMEM_LIMIT_BYTES),
        name="channel_mix",
    )(x2d, w1, w2, g2, b2)


def kernel(x, w_in, b_gate, conv_w, v_norm_g, v_norm_b, w_s, b_s, w_pa, w_pb, w_o,
           ln1_g, ln1_b, w_ff1, w_ff2, ln2_g, ln2_b):
    bsz, seq, d = x.shape
    depth = w_in.shape[0]
    alpha = (2.0 * depth) ** 0.25
    bf = jnp.bfloat16
    row = lambda p: p.reshape(1, -1)
    outs = []
    for b in range(bsz):
        xb = x[b]
        for l in range(depth):
            bs_full = jnp.repeat(jnp.transpose(b_s[l]), DH_B, axis=1)
            xb, w1b, w2b = _mix_layer(
                xb, w_in[l].astype(bf), row(b_gate[l]), conv_w[l], row(v_norm_g[l]),
                row(v_norm_b[l]), w_s[l].astype(bf), bs_full, w_pa[l].astype(bf),
                w_pb[l].astype(bf), w_o[l].astype(bf), row(ln1_g[l]), row(ln1_b[l]),
                w_ff1[l], w_ff2[l], alpha=alpha)
            xb = _ffn_layer(xb, w1b, w2b, row(ln2_g[l]), row(ln2_b[l]), alpha=alpha)
        outs.append(xb)
    return jnp.stack(outs, axis=0) if bsz > 1 else outs[0][None]
```

```python
import functools

import jax
import jax.numpy as jnp
from jax import lax
from jax.experimental import pallas as pl
from jax.experimental.pallas import tpu as pltpu

CONV_K = 3
CHUNK = 128
DH_B = 128
LN_EPS = 1e-5

LANES = 128
BF16_ROWS = 16
HALO = BF16_ROWS
VMEM_LIMIT_BYTES = 60 * 1024 * 1024

SEQ_TILE_MIX = 512
SEQ_TILE_FFN = 512
FF_CHUNK = 1024
CH_PARTS = 3
V_PARTS = 4
NORM_BLOCKS_MIX = 2 * CH_PARTS + V_PARTS - 2
NORM_BLOCKS_FFN = 8


def _layernorm(y, g, b):
    mu = jnp.mean(y, axis=-1, keepdims=True)
    yc = y - mu
    var = jnp.mean(yc * yc, axis=-1, keepdims=True)
    return yc * lax.rsqrt(var + LN_EPS) * g + b


def _sigmoid(z):
    return 0.5 * jnp.tanh(0.5 * z) + 0.5


def _dot(a, b):
    return jnp.dot(a, b, preferred_element_type=jnp.float32)


def _norm_carried_tile(carry_ref, o_ref, g_ref, b_ref, n_blocks):
    rows = carry_ref.shape[0] // n_blocks
    assert rows * n_blocks == carry_ref.shape[0] and rows % BF16_ROWS == 0
    d = carry_ref.shape[1]
    tokens = []
    for r in range(n_blocks):
        y = _layernorm(carry_ref[pl.ds(r * rows, rows), :], g_ref[...], b_ref[...])
        o_ref[pl.ds(r * rows, rows), :] = y
        t = y.reshape(rows // BF16_ROWS, BF16_ROWS, d).sum(axis=0)
        t = functools.reduce(
            lambda p, q: p + q, [t[:, c * LANES:(c + 1) * LANES] for c in range(d // LANES)])
        tokens.append(t.astype(jnp.bfloat16))
    return tokens


def _tie(value, token):
    never = pl.program_id(0) < 0
    r, c = token.shape
    head = jnp.where(never, token, value[0:r, 0:c])
    top = jnp.concatenate([head, value[0:r, c:]], axis=1)
    return jnp.concatenate([top, value[r:]], axis=0)


def _deferred_norm_steps(carry_ref, o_ref, g_ref, b_ref, n_blocks, tile_body):
    i = pl.program_id(0)
    n_tiles = pl.num_programs(0) - 1

    @pl.when(i == 0)
    def _():
        carry_ref[...] = jnp.zeros_like(carry_ref)

    @pl.when(i < n_tiles)
    def _():
        tile_body(_norm_carried_tile(carry_ref, o_ref, g_ref, b_ref, n_blocks))

    @pl.when(i == n_tiles)
    def _():
        _norm_carried_tile(carry_ref, o_ref, g_ref, b_ref, n_blocks)


def _mix_kernel(xprev_ref, x_ref, xnext_ref, w_in_ref, bgate_ref, convw_ref,
                vg_ref, vb_ref, ws_ref, bs_ref, wpa_ref, wpb_ref, wo_ref,
                g1_ref, b1_ref, w1f_ref, w2f_ref, o_ref, w1b_ref, w2b_ref,
                xe_ref, ch_ref, carry_ref, *, alpha, w_a, w_b, d_model):
    ts = x_ref.shape[0]
    i = pl.program_id(0)
    n_tiles = pl.num_programs(0) - 1
    off_ca = w_a
    off_ha = 2 * w_a
    off_ub = 3 * w_a
    off_vb = off_ub + w_b
    off_ga = off_vb + w_b
    off_gb = off_ga + d_model

    def tile_body(tokens):
        w1b_ref[...] = w1f_ref[...].astype(jnp.bfloat16)
        w2b_ref[...] = w2f_ref[...].astype(jnp.bfloat16)

        x = x_ref[...]
        xe_ref[pl.ds(0, HALO), :] = jnp.where(i > 0, xprev_ref[...], 0.0).astype(jnp.bfloat16)
        xe_ref[pl.ds(HALO, ts), :] = x.astype(jnp.bfloat16)
        xe_ref[pl.ds(HALO + ts, HALO), :] = jnp.where(
            i < n_tiles - 1, xnext_ref[...], 0.0).astype(jnp.bfloat16)

        xe = xe_ref[...]
        xm = xe_ref[pl.ds(HALO, ts), :]
        n_chunks = ts // CHUNK
        n_heads = w_b // DH_B

        tok = iter(tokens)
        tied = lambda lhs, first: lhs if first else _tie(lhs, next(tok))
        c_parts = [(off_ca + p * (w_a // CH_PARTS), w_a // CH_PARTS) for p in range(CH_PARTS)]
        h_parts = [(off_ha + p * (w_a // CH_PARTS), w_a // CH_PARTS) for p in range(CH_PARTS)]
        ch_order = [q for pair in zip(c_parts, h_parts) for q in pair]
        v_parts = [(off_vb + q * (w_b // V_PARTS), w_b // V_PARTS) for q in range(V_PARTS)]
        ch_out, v_out = [], []
        for j, (off, width) in enumerate(ch_order):
            ch_out.append(_dot(tied(xe, j == 0), w_in_ref[:, off:off + width]))
            if j < V_PARTS:
                off, width = v_parts[j]
                v_out.append(jax.nn.gelu(_dot(tied(xm, j == 0), w_in_ref[:, off:off + width])))
        assert next(tok, None) is None
        for p in range(CH_PARTS):
            width = w_a // CH_PARTS
            ch_ref[:, p * width:(p + 1) * width] = ch_out[2 * p] * ch_out[2 * p + 1]
        v = _layernorm(jnp.concatenate(v_out, axis=1), vg_ref[...], vb_ref[...])
        v = v.astype(jnp.bfloat16)
        u = jax.nn.gelu(_dot(xm, w_in_ref[:, off_ub:off_vb]))
        b_a = _dot(xm, w_in_ref[:, 0:off_ca])

        head_cols = []
        for h in range(n_heads):
            vh = jnp.concatenate(
                [v[k * CHUNK:(k + 1) * CHUNK, h * DH_B:(h + 1) * DH_B] for k in range(n_chunks)],
                axis=1)
            mh = _dot(ws_ref[h], vh)
            mh = mh + jnp.concatenate([bs_ref[:, h * DH_B:(h + 1) * DH_B]] * n_chunks, axis=1)
            head_cols.append(jnp.concatenate(
                [mh[:, k * DH_B:(k + 1) * DH_B] for k in range(n_chunks)], axis=0))
        mixed = jnp.concatenate(head_cols, axis=1)

        cw = convw_ref[...]
        conv = (ch_ref[pl.ds(HALO - 1, ts), :] * cw[0:1, :]
                + ch_ref[pl.ds(HALO, ts), :] * cw[1:2, :]
                + ch_ref[pl.ds(HALO + 1, ts), :] * cw[2:3, :])
        a = (b_a * conv).astype(jnp.bfloat16)
        g_b = _sigmoid(_dot(xm, w_in_ref[:, off_gb:off_gb + d_model])
                       + bgate_ref[:, d_model:2 * d_model])
        y_b =_dot((u * mixed).astype(jnp.bfloat16), wpb_ref[...])
        g_a = _sigmoid(_dot(xm, w_in_ref[:, off_ga:off_gb]) + bgate_ref[:, 0:d_model])
        acc = g_b * y_b
        y_a = _dot(a, wpa_ref[...])
        acc = acc + g_a * y_a
        carry_ref[...] = alpha * x + _dot(acc.astype(jnp.bfloat16), wo_ref[...])

    _deferred_norm_steps(carry_ref, o_ref, g1_ref, b1_ref, NORM_BLOCKS_MIX, tile_body)


def _ffn_kernel(x_ref, w1_ref, w2_ref, g2_ref, b2_ref, o_ref, carry_ref, *, alpha):
    n_ff = w1_ref.shape[1] // FF_CHUNK
    assert NORM_BLOCKS_FFN == 2 * n_ff

    def tile_body(tokens):
        x = x_ref[...]
        xb = x.astype(jnp.bfloat16)
        y = alpha * x
        for k in range(n_ff):
            lhs = xb if k == 0 else _tie(xb, tokens[2 * k - 1])
            hid = jnp.maximum(_dot(lhs, w1_ref[:, k * FF_CHUNK:(k + 1) * FF_CHUNK]), 0.0)
            hid = _tie((hid * hid).astype(jnp.bfloat16), tokens[2 * k])
            if k == n_ff - 1:
                hid = _tie(hid, tokens[2 * k + 1])
            y = y + _dot(hid, w2_ref[k * FF_CHUNK:(k + 1) * FF_CHUNK, :])
        carry_ref[...] = y

    _deferred_norm_steps(carry_ref, o_ref, g2_ref, b2_ref, NORM_BLOCKS_FFN, tile_body)


def _resident(shape):
    nd = len(shape)
    return pl.BlockSpec(shape, lambda i: (0,) * nd, pipeline_mode=pl.Buffered(1))


def _mix_layer(x2d, w_in, b_gate, conv_w, vg, vb, w_s, bs_full, w_pa, w_pb, w_o, g1, b1,
               w_ff1, w_ff2, *, alpha):
    s, d = x2d.shape
    w_a = w_pa.shape[0]
    w_b = w_pb.shape[0]
    d_ff = w_ff1.shape[1]
    ts = SEQ_TILE_MIX
    assert s % ts == 0 and ts % CHUNK == 0
    n_tiles = s // ts
    hb = ts // HALO
    n_halo_blocks = s // HALO
    r1, r2 = d // n_tiles, d_ff // n_tiles
    assert r1 * n_tiles == d and r2 * n_tiles == d_ff
    assert r1 % BF16_ROWS == 0 and r2 % BF16_ROWS == 0
    last = n_tiles - 1
    tile = lambda i: jnp.minimum(i, last)
    kern = functools.partial(_mix_kernel, alpha=alpha, w_a=w_a, w_b=w_b, d_model=d)
    return pl.pallas_call(
        kern,
        grid=(n_tiles + 1,),
        in_specs=[
            pl.BlockSpec((HALO, d), lambda i: (jnp.maximum(tile(i) * hb - 1, 0), 0)),
            pl.BlockSpec((ts, d), lambda i: (tile(i), 0)),
            pl.BlockSpec((HALO, d),
                         lambda i: (jnp.minimum((tile(i) + 1) * hb, n_halo_blocks - 1), 0)),
            _resident(w_in.shape), _resident(b_gate.shape), _resident(conv_w.shape),
            _resident(vg.shape), _resident(vb.shape), _resident(w_s.shape),
            _resident(bs_full.shape), _resident(w_pa.shape), _resident(w_pb.shape),
            _resident(w_o.shape), _resident(g1.shape), _resident(b1.shape),
            pl.BlockSpec((r1, d_ff), lambda i: (tile(i), 0)),
            pl.BlockSpec((r2, d), lambda i: (tile(i), 0)),
        ],
        out_specs=[
            pl.BlockSpec((ts, d), lambda i: (jnp.maximum(i - 1, 0), 0)),
            pl.BlockSpec((r1, d_ff), lambda i: (tile(i), 0)),
            pl.BlockSpec((r2, d), lambda i: (tile(i), 0)),
        ],
        out_shape=[
            jax.ShapeDtypeStruct((s, d), jnp.float32),
            jax.ShapeDtypeStruct(w_ff1.shape, jnp.bfloat16),
            jax.ShapeDtypeStruct(w_ff2.shape, jnp.bfloat16),
        ],
        scratch_shapes=[
            pltpu.VMEM((ts + 2 * HALO, d), jnp.bfloat16),
            pltpu.VMEM((ts + 2 * HALO, w_a), jnp.float32),
            pltpu.VMEM((ts, d), jnp.float32),
        ],
        compiler_params=pltpu.CompilerParams(
            dimension_semantics=("arbitrary",), vmem_limit_bytes=VMEM_LIMIT_BYTES),
        name="token_mix",
    )(x2d, x2d, x2d, w_in, b_gate, conv_w, vg, vb, w_s, bs_full, w_pa, w_pb, w_o, g1, b1,
      w_ff1, w_ff2)


def _ffn_layer(x2d, w1, w2, g2, b2, *, alpha):
    s, d = x2d.shape
    ts = SEQ_TILE_FFN
    assert s % ts == 0
    n_tiles = s // ts
    return pl.pallas_call(
        functools.partial(_ffn_kernel, alpha=alpha),
        grid=(n_tiles + 1,),
        in_specs=[
            pl.BlockSpec((ts, d), lambda i: (jnp.minimum(i, n_tiles - 1), 0)),
            _resident(w1.shape), _resident(w2.shape), _resident(g2.shape), _resident(b2.shape),
        ],
        out_specs=pl.BlockSpec((ts, d), lambda i: (jnp.maximum(i - 1, 0), 0)),
        out_shape=jax.ShapeDtypeStruct((s, d), jnp.float32),
        scratch_shapes=[pltpu.VMEM((ts, d), jnp.float32)],
        compiler_params=pltpu.CompilerParams(
            dimension_semantics=("arbitrary",), vmem_limit_bytes=VMEM_LIMIT_BYTES),
        name="channel_mix",
    )(x2d, w1, w2, g2, b2)


def kernel(x, w_in, b_gate, conv_w, v_norm_g, v_norm_b, w_s, b_s, w_pa, w_pb, w_o,
           ln1_g, ln1_b, w_ff1, w_ff2, ln2_g, ln2_b):
    bsz, seq, d = x.shape
    depth = w_in.shape[0]
    alpha = (2.0 * depth) ** 0.25
    bf = jnp.bfloat16
    row = lambda p: p.reshape(1, -1)
    outs = []
    for b in range(bsz):
        xb = x[b]
        for l in range(depth):
            bs_full = jnp.repeat(jnp.transpose(b_s[l]), DH_B, axis=1)
            xb, w1b, w2b = _mix_layer(
                xb, w_in[l].astype(bf), row(b_gate[l]), conv_w[l], row(v_norm_g[l]),
                row(v_norm_b[l]), w_s[l].astype(bf), bs_full, w_pa[l].astype(bf),
                w_pb[l].astype(bf), w_o[l].astype(bf), row(ln1_g[l]), row(ln1_b[l]),
                w_ff1[l], w_ff2[l], alpha=alpha)
            xb = _ffn_layer(xb, w1b, w2b, row(ln2_g[l]), row(ln2_b[l]), alpha=alpha)
        outs.append(xb)
    return jnp.stack(outs, axis=0) if bsz > 1 else outs[0][None]
```

```python
import functools

import jax
import jax.numpy as jnp
from jax import lax
from jax.experimental import pallas as pl
from jax.experimental.pallas import tpu as pltpu

CONV_K = 3
CHUNK = 128
DH_B = 128
LN_EPS = 1e-5

LANES = 128
BF16_ROWS = 16
HALO = BF16_ROWS
VMEM_LIMIT_BYTES = 60 * 1024 * 1024

SEQ_TILE_MIX = 512
SEQ_TILE_FFN = 512
FF_CHUNK = 1024
CH_PARTS = 3
V_PARTS = 4
NORM_BLOCKS_MIX = 2 * CH_PARTS + V_PARTS - 2
NORM_BLOCKS_FFN = 8
STAGE_SLOTS = 3
STAGE_ROWS_W_IN = 32
STAGE_ROWS_PROJ = 256


def _layernorm(y, g, b):
    mu = jnp.mean(y, axis=-1, keepdims=True)
    yc = y - mu
    var = jnp.mean(yc * yc, axis=-1, keepdims=True)
    return yc * lax.rsqrt(var + LN_EPS) * g + b


def _sigmoid(z):
    return 0.5 * jnp.tanh(0.5 * z) + 0.5


def _dot(a, b):
    return jnp.dot(a, b, preferred_element_type=jnp.float32)


def _norm_carried_tile(carry_ref, o_ref, g_ref, b_ref, n_blocks):
    rows = carry_ref.shape[0] // n_blocks
    assert rows * n_blocks == carry_ref.shape[0] and rows % BF16_ROWS == 0
    d = carry_ref.shape[1]
    tokens = []
    for r in range(n_blocks):
        y = _layernorm(carry_ref[pl.ds(r * rows, rows), :], g_ref[...], b_ref[...])
        o_ref[pl.ds(r * rows, rows), :] = y
        t = y.reshape(rows // BF16_ROWS, BF16_ROWS, d).sum(axis=0)
        t = functools.reduce(
            lambda p, q: p + q, [t[:, c * LANES:(c + 1) * LANES] for c in range(d // LANES)])
        tokens.append(t.astype(jnp.bfloat16))
    return tokens


def _tie(value, token):
    never = pl.program_id(0) < 0
    r, c = token.shape
    head = jnp.where(never, token, value[0:r, 0:c])
    top = jnp.concatenate([head, value[0:r, c:]], axis=1)
    return jnp.concatenate([top, value[r:]], axis=0)


def _stage_weights_bf16(families):
    lookahead = STAGE_SLOTS - 1

    def staged(*scratch):
        plans = []
        for f, (pairs, rows) in enumerate(families):
            stage, sem = scratch[2 * f], scratch[2 * f + 1]
            chunks = []
            for src, dst in pairs:
                assert src.shape == dst.shape and src.shape[0] % rows == 0
                for r0 in range(0, src.shape[0], rows):
                    k = len(chunks)
                    copy = pltpu.make_async_copy(
                        src.at[pl.ds(r0, rows), :], stage.at[k % STAGE_SLOTS],
                        sem.at[k % STAGE_SLOTS])
                    chunks.append((copy, stage.at[k % STAGE_SLOTS], dst.at[pl.ds(r0, rows), :]))
            plans.append(chunks)
        for chunks in plans:
            for copy, _, _ in chunks[:lookahead]:
                copy.start()
        for chunks in plans:
            for k, (copy, stage_slot, dst_rows) in enumerate(chunks):
                if k + lookahead < len(chunks):
                    chunks[k + lookahead][0].start()
                copy.wait()
                dst_rows[...] = stage_slot[...].astype(jnp.bfloat16)

    scratch_types = []
    for pairs, rows in families:
        cols = pairs[0][0].shape[1]
        scratch_types += [pltpu.VMEM((STAGE_SLOTS, rows, cols), jnp.float32),
                          pltpu.SemaphoreType.DMA((STAGE_SLOTS,))]
    pl.run_scoped(staged, *scratch_types)


def _deferred_norm_steps(carry_ref, o_ref, g_ref, b_ref, n_blocks, tile_body, setup=None):
    i = pl.program_id(0)
    n_tiles = pl.num_programs(0) - 1

    @pl.when(i == 0)
    def _():
        carry_ref[...] = jnp.zeros_like(carry_ref)
        if setup is not None:
            setup()

    @pl.when(i < n_tiles)
    def _():
        tile_body(_norm_carried_tile(carry_ref, o_ref, g_ref, b_ref, n_blocks))

    @pl.when(i == n_tiles)
    def _():
        _norm_carried_tile(carry_ref, o_ref, g_ref, b_ref, n_blocks)


def _mix_kernel(xprev_ref, x_ref, xnext_ref, w_in_hbm, bgate_ref, convw_ref,
                vg_ref, vb_ref, ws_ref, bs_ref, wpa_hbm, wpb_hbm, wo_hbm,
                g1_ref, b1_ref, w1f_ref, w2f_ref, o_ref, w1b_ref, w2b_ref,
                w_in_ref, wpa_ref, wpb_ref, wo_ref, xe_ref, ch_ref, carry_ref,
                *, alpha, w_a, w_b, d_model):
    ts = x_ref.shape[0]

    def stage_weights():
        _stage_weights_bf16([
            ([(w_in_hbm, w_in_ref)], STAGE_ROWS_W_IN),
            ([(wpa_hbm, wpa_ref), (wpb_hbm, wpb_ref), (wo_hbm, wo_ref)], STAGE_ROWS_PROJ),
        ])

    i = pl.program_id(0)
    n_tiles = pl.num_programs(0) - 1
    off_ca = w_a
    off_ha = 2 * w_a
    off_ub = 3 * w_a
    off_vb = off_ub + w_b
    off_ga = off_vb + w_b
    off_gb = off_ga + d_model

    def tile_body(tokens):
        w1b_ref[...] = w1f_ref[...].astype(jnp.bfloat16)
        w2b_ref[...] = w2f_ref[...].astype(jnp.bfloat16)

        x = x_ref[...]
        xe_ref[pl.ds(0, HALO), :] = jnp.where(i > 0, xprev_ref[...], 0.0).astype(jnp.bfloat16)
        xe_ref[pl.ds(HALO, ts), :] = x.astype(jnp.bfloat16)
        xe_ref[pl.ds(HALO + ts, HALO), :] = jnp.where(
            i < n_tiles - 1, xnext_ref[...], 0.0).astype(jnp.bfloat16)

        xe = xe_ref[...]
        xm = xe_ref[pl.ds(HALO, ts), :]
        n_chunks = ts // CHUNK
        n_heads = w_b // DH_B

        tok = iter(tokens)
        tied = lambda lhs, first: lhs if first else _tie(lhs, next(tok))
        c_parts = [(off_ca + p * (w_a // CH_PARTS), w_a // CH_PARTS) for p in range(CH_PARTS)]
        h_parts = [(off_ha + p * (w_a // CH_PARTS), w_a // CH_PARTS) for p in range(CH_PARTS)]
        ch_order = [q for pair in zip(c_parts, h_parts) for q in pair]
        v_parts = [(off_vb + q * (w_b // V_PARTS), w_b // V_PARTS) for q in range(V_PARTS)]
        ch_out, v_out = [], []
        for j, (off, width) in enumerate(ch_order):
            ch_out.append(_dot(tied(xe, j == 0), w_in_ref[:, off:off + width]))
            if j < V_PARTS:
                off, width = v_parts[j]
                v_out.append(jax.nn.gelu(_dot(tied(xm, j == 0), w_in_ref[:, off:off + width])))
        assert next(tok, None) is None
        for p in range(CH_PARTS):
            width = w_a // CH_PARTS
            ch_ref[:, p * width:(p + 1) * width] = ch_out[2 * p] * ch_out[2 * p + 1]
        v = _layernorm(jnp.concatenate(v_out, axis=1), vg_ref[...], vb_ref[...])
        v = v.astype(jnp.bfloat16)
        u = jax.nn.gelu(_dot(xm, w_in_ref[:, off_ub:off_vb]))
        b_a = _dot(xm, w_in_ref[:, 0:off_ca])

        head_cols = []
        for h in range(n_heads):
            vh = jnp.concatenate(
                [v[k * CHUNK:(k + 1) * CHUNK, h * DH_B:(h + 1) * DH_B] for k in range(n_chunks)],
                axis=1)
            mh = _dot(ws_ref[h], vh)
            mh = mh + jnp.concatenate([bs_ref[:, h * DH_B:(h + 1) * DH_B]] * n_chunks, axis=1)
            head_cols.append(jnp.concatenate(
                [mh[:, k * DH_B:(k + 1) * DH_B] for k in range(n_chunks)], axis=0))
        mixed = jnp.concatenate(head_cols, axis=1)

        cw = convw_ref[...]
        conv = (ch_ref[pl.ds(HALO - 1, ts), :] * cw[0:1, :]
                + ch_ref[pl.ds(HALO, ts), :] * cw[1:2, :]
                + ch_ref[pl.ds(HALO + 1, ts), :] * cw[2:3, :])
        a = (b_a * conv).astype(jnp.bfloat16)
        g_b = _sigmoid(_dot(xm, w_in_ref[:, off_gb:off_gb + d_model])
                       + bgate_ref[:, d_model:2 * d_model])
        y_b =_dot((u * mixed).astype(jnp.bfloat16), wpb_ref[...])
        g_a = _sigmoid(_dot(xm, w_in_ref[:, off_ga:off_gb]) + bgate_ref[:, 0:d_model])
        acc = g_b * y_b
        y_a = _dot(a, wpa_ref[...])
        acc = acc + g_a * y_a
        carry_ref[...] = alpha * x + _dot(acc.astype(jnp.bfloat16), wo_ref[...])

    _deferred_norm_steps(carry_ref, o_ref, g1_ref, b1_ref, NORM_BLOCKS_MIX, tile_body,
                         setup=stage_weights)


def _ffn_kernel(x_ref, w1_ref, w2_ref, g2_ref, b2_ref, o_ref, carry_ref, *, alpha):
    n_ff = w1_ref.shape[1] // FF_CHUNK
    assert NORM_BLOCKS_FFN == 2 * n_ff

    def tile_body(tokens):
        x = x_ref[...]
        xb = x.astype(jnp.bfloat16)
        y = alpha * x
        for k in range(n_ff):
            lhs = xb if k == 0 else _tie(xb, tokens[2 * k - 1])
            hid = jnp.maximum(_dot(lhs, w1_ref[:, k * FF_CHUNK:(k + 1) * FF_CHUNK]), 0.0)
            hid = _tie((hid * hid).astype(jnp.bfloat16), tokens[2 * k])
            if k == n_ff - 1:
                hid = _tie(hid, tokens[2 * k + 1])
            y = y + _dot(hid, w2_ref[k * FF_CHUNK:(k + 1) * FF_CHUNK, :])
        carry_ref[...] = y

    _deferred_norm_steps(carry_ref, o_ref, g2_ref, b2_ref, NORM_BLOCKS_FFN, tile_body)


def _resident(shape):
    nd = len(shape)
    return pl.BlockSpec(shape, lambda i: (0,) * nd, pipeline_mode=pl.Buffered(1))


def _mix_layer(x2d, w_in, b_gate, conv_w, vg, vb, w_s, bs_full, w_pa, w_pb, w_o, g1, b1,
               w_ff1, w_ff2, *, alpha):
    s, d = x2d.shape
    w_a = w_pa.shape[0]
    w_b = w_pb.shape[0]
    d_ff = w_ff1.shape[1]
    ts = SEQ_TILE_MIX
    assert s % ts == 0 and ts % CHUNK == 0
    n_tiles = s // ts
    hb = ts // HALO
    n_halo_blocks = s // HALO
    r1, r2 = d // n_tiles, d_ff // n_tiles
    assert r1 * n_tiles == d and r2 * n_tiles == d_ff
    assert r1 % BF16_ROWS == 0 and r2 % BF16_ROWS == 0
    last = n_tiles - 1
    tile = lambda i: jnp.minimum(i, last)
    kern = functools.partial(_mix_kernel, alpha=alpha, w_a=w_a, w_b=w_b, d_model=d)
    in_hbm = pl.BlockSpec(memory_space=pl.ANY)
    return pl.pallas_call(
        kern,
        grid=(n_tiles + 1,),
        in_specs=[
            pl.BlockSpec((HALO, d), lambda i: (jnp.maximum(tile(i) * hb - 1, 0), 0)),
            pl.BlockSpec((ts, d), lambda i: (tile(i), 0)),
            pl.BlockSpec((HALO, d),
                         lambda i: (jnp.minimum((tile(i) + 1) * hb, n_halo_blocks - 1), 0)),
            in_hbm, _resident(b_gate.shape), _resident(conv_w.shape),
            _resident(vg.shape), _resident(vb.shape), _resident(w_s.shape),
            _resident(bs_full.shape), in_hbm, in_hbm, in_hbm,
            _resident(g1.shape), _resident(b1.shape),
            pl.BlockSpec((r1, d_ff), lambda i: (tile(i), 0)),
            pl.BlockSpec((r2, d), lambda i: (tile(i), 0)),
        ],
        out_specs=[
            pl.BlockSpec((ts, d), lambda i: (jnp.maximum(i - 1, 0), 0)),
            pl.BlockSpec((r1, d_ff), lambda i: (tile(i), 0)),
            pl.BlockSpec((r2, d), lambda i: (tile(i), 0)),
        ],
        out_shape=[
            jax.ShapeDtypeStruct((s, d), jnp.float32),
            jax.ShapeDtypeStruct(w_ff1.shape, jnp.bfloat16),
            jax.ShapeDtypeStruct(w_ff2.shape, jnp.bfloat16),
        ],
        scratch_shapes=[
            pltpu.VMEM(w_in.shape, jnp.bfloat16),
            pltpu.VMEM(w_pa.shape, jnp.bfloat16),
            pltpu.VMEM(w_pb.shape, jnp.bfloat16),
            pltpu.VMEM(w_o.shape, jnp.bfloat16),
            pltpu.VMEM((ts + 2 * HALO, d), jnp.bfloat16),
            pltpu.VMEM((ts + 2 * HALO, w_a), jnp.float32),
            pltpu.VMEM((ts, d), jnp.float32),
        ],
        compiler_params=pltpu.CompilerParams(
            dimension_semantics=("arbitrary",), vmem_limit_bytes=VMEM_LIMIT_BYTES),
        name="token_mix",
    )(x2d, x2d, x2d, w_in, b_gate, conv_w, vg, vb, w_s, bs_full, w_pa, w_pb, w_o, g1, b1,
      w_ff1, w_ff2)


def _ffn_layer(x2d, w1, w2, g2, b2, *, alpha):
    s, d = x2d.shape
    ts = SEQ_TILE_FFN
    assert s % ts == 0
    n_tiles = s // ts
    return pl.pallas_call(
        functools.partial(_ffn_kernel, alpha=alpha),
        grid=(n_tiles + 1,),
        in_specs=[
            pl.BlockSpec((ts, d), lambda i: (jnp.minimum(i, n_tiles - 1), 0)),
            _resident(w1.shape), _resident(w2.shape), _resident(g2.shape), _resident(b2.shape),
        ],
        out_specs=pl.BlockSpec((ts, d), lambda i: (jnp.maximum(i - 1, 0), 0)),
        out_shape=jax.ShapeDtypeStruct((s, d), jnp.float32),
        scratch_shapes=[pltpu.VMEM((ts, d), jnp.float32)],
        compiler_params=pltpu.CompilerParams(
            dimension_semantics=("arbitrary",), vmem_limit_bytes=VMEM_LIMIT_BYTES),
        name="channel_mix",
    )(x2d, w1, w2, g2, b2)


def kernel(x, w_in, b_gate, conv_w, v_norm_g, v_norm_b, w_s, b_s, w_pa, w_pb, w_o,
           ln1_g, ln1_b, w_ff1, w_ff2, ln2_g, ln2_b):
    bsz, seq, d = x.shape
    depth = w_in.shape[0]
    alpha = (2.0 * depth) ** 0.25
    bf = jnp.bfloat16
    row = lambda p: p.reshape(1, -1)
    outs = []
    for b in range(bsz):
        xb = x[b]
        for l in range(depth):
            bs_full = jnp.repeat(jnp.transpose(b_s[l]), DH_B, axis=1)
            xb, w1b, w2b = _mix_layer(
                xb, w_in[l], row(b_gate[l]), conv_w[l], row(v_norm_g[l]),
                row(v_norm_b[l]), w_s[l].astype(bf), bs_full, w_pa[l], w_pb[l], w_o[l],
                row(ln1_g[l]), row(ln1_b[l]),
                w_ff1[l], w_ff2[l], alpha=alpha)
            xb = _ffn_layer(xb, w1b, w2b, row(ln2_g[l]), row(ln2_b[l]), alpha=alpha)
        outs.append(xb)
    return jnp.stack(outs, axis=0) if bsz > 1 else outs[0][None]
```

```python
import functools

import jax
import jax.numpy as jnp
from jax import lax
from jax.experimental import pallas as pl
from jax.experimental.pallas import tpu as pltpu

CONV_K = 3
CHUNK = 128
DH_B = 128
LN_EPS = 1e-5

LANES = 128
BF16_ROWS = 16
HALO = BF16_ROWS
VMEM_LIMIT_BYTES = 60 * 1024 * 1024

SEQ_TILE_MIX = 512
SEQ_TILE_FFN = 512
FF_CHUNK = 1024
CH_PARTS = 3
V_PARTS = 4
NORM_BLOCKS_MIX = 2 * CH_PARTS + V_PARTS - 2
NORM_BLOCKS_FFN = 8
STAGE_SLOTS = 4
STAGE_ROWS_W_IN = 32
STAGE_ROWS_PROJ = 256


def _layernorm(y, g, b):
    mu = jnp.mean(y, axis=-1, keepdims=True)
    yc = y - mu
    var = jnp.mean(yc * yc, axis=-1, keepdims=True)
    return yc * lax.rsqrt(var + LN_EPS) * g + b


def _sigmoid(z):
    return 0.5 * jnp.tanh(0.5 * z) + 0.5


def _dot(a, b):
    return jnp.dot(a, b, preferred_element_type=jnp.float32)


def _norm_carried_tile(carry_ref, o_ref, g_ref, b_ref, n_blocks):
    rows = carry_ref.shape[0] // n_blocks
    assert rows * n_blocks == carry_ref.shape[0] and rows % BF16_ROWS == 0
    d = carry_ref.shape[1]
    tokens = []
    for r in range(n_blocks):
        y = _layernorm(carry_ref[pl.ds(r * rows, rows), :], g_ref[...], b_ref[...])
        o_ref[pl.ds(r * rows, rows), :] = y
        t = y.reshape(rows // BF16_ROWS, BF16_ROWS, d).sum(axis=0)
        t = functools.reduce(
            lambda p, q: p + q, [t[:, c * LANES:(c + 1) * LANES] for c in range(d // LANES)])
        tokens.append(t.astype(jnp.bfloat16))
    return tokens


def _tie(value, token):
    never = pl.program_id(0) < 0
    r, c = token.shape
    head = jnp.where(never, token, value[0:r, 0:c])
    top = jnp.concatenate([head, value[0:r, c:]], axis=1)
    return jnp.concatenate([top, value[r:]], axis=0)


def _stage_weights_bf16(families):
    lookahead = STAGE_SLOTS - 1

    def staged(*scratch):
        plans = []
        for f, (pairs, rows) in enumerate(families):
            stage, sem = scratch[2 * f], scratch[2 * f + 1]
            chunks = []
            for src, dst in pairs:
                assert src.shape == dst.shape and src.shape[0] % rows == 0
                for r0 in range(0, src.shape[0], rows):
                    k = len(chunks)
                    copy = pltpu.make_async_copy(
                        src.at[pl.ds(r0, rows), :], stage.at[k % STAGE_SLOTS],
                        sem.at[k % STAGE_SLOTS])
                    chunks.append((copy, stage.at[k % STAGE_SLOTS], dst.at[pl.ds(r0, rows), :]))
            plans.append(chunks)
        for chunks in plans:
            for k in range(min(lookahead, len(chunks))):
                chunks[k][0].start(priority=k % 2)
        for chunks in plans:
            for k, (copy, stage_slot, dst_rows) in enumerate(chunks):
                if k + lookahead < len(chunks):
                    chunks[k + lookahead][0].start(priority=(k + lookahead) % 2)
                copy.wait()
                dst_rows[...] = stage_slot[...].astype(jnp.bfloat16)

    scratch_types = []
    for pairs, rows in families:
        cols = pairs[0][0].shape[1]
        scratch_types += [pltpu.VMEM((STAGE_SLOTS, rows, cols), jnp.float32),
                          pltpu.SemaphoreType.DMA((STAGE_SLOTS,))]
    pl.run_scoped(staged, *scratch_types)


def _deferred_norm_steps(carry_ref, o_ref, g_ref, b_ref, n_blocks, tile_body, setup=None):
    i = pl.program_id(0)
    n_tiles = pl.num_programs(0) - 1

    @pl.when(i == 0)
    def _():
        carry_ref[...] = jnp.zeros_like(carry_ref)
        if setup is not None:
            setup()

    @pl.when(i < n_tiles)
    def _():
        tile_body(_norm_carried_tile(carry_ref, o_ref, g_ref, b_ref, n_blocks))

    @pl.when(i == n_tiles)
    def _():
        _norm_carried_tile(carry_ref, o_ref, g_ref, b_ref, n_blocks)


def _mix_kernel(xprev_ref, x_ref, xnext_ref, w_in_hbm, bgate_ref, convw_ref,
                vg_ref, vb_ref, ws_ref, bs_ref, wpa_hbm, wpb_hbm, wo_hbm,
                g1_ref, b1_ref, w1f_ref, w2f_ref, o_ref, w1b_ref, w2b_ref,
                w_in_ref, wpa_ref, wpb_ref, wo_ref, xe_ref, ch_ref, carry_ref,
                *, alpha, w_a, w_b, d_model):
    ts = x_ref.shape[0]

    def stage_weights():
        _stage_weights_bf16([
            ([(w_in_hbm, w_in_ref)], STAGE_ROWS_W_IN),
            ([(wpa_hbm, wpa_ref), (wpb_hbm, wpb_ref), (wo_hbm, wo_ref)], STAGE_ROWS_PROJ),
        ])

    i = pl.program_id(0)
    n_tiles = pl.num_programs(0) - 1
    off_ca = w_a
    off_ha = 2 * w_a
    off_ub = 3 * w_a
    off_vb = off_ub + w_b
    off_ga = off_vb + w_b
    off_gb = off_ga + d_model

    def tile_body(tokens):
        w1b_ref[...] = w1f_ref[...].astype(jnp.bfloat16)
        w2b_ref[...] = w2f_ref[...].astype(jnp.bfloat16)

        x = x_ref[...]
        xe_ref[pl.ds(0, HALO), :] = jnp.where(i > 0, xprev_ref[...], 0.0).astype(jnp.bfloat16)
        xe_ref[pl.ds(HALO, ts), :] = x.astype(jnp.bfloat16)
        xe_ref[pl.ds(HALO + ts, HALO), :] = jnp.where(
            i < n_tiles - 1, xnext_ref[...], 0.0).astype(jnp.bfloat16)

        xe = xe_ref[...]
        xm = xe_ref[pl.ds(HALO, ts), :]
        n_chunks = ts // CHUNK
        n_heads = w_b // DH_B

        tok = iter(tokens)
        tied = lambda lhs, first: lhs if first else _tie(lhs, next(tok))
        c_parts = [(off_ca + p * (w_a // CH_PARTS), w_a // CH_PARTS) for p in range(CH_PARTS)]
        h_parts = [(off_ha + p * (w_a // CH_PARTS), w_a // CH_PARTS) for p in range(CH_PARTS)]
        ch_order = [q for pair in zip(c_parts, h_parts) for q in pair]
        v_parts = [(off_vb + q * (w_b // V_PARTS), w_b // V_PARTS) for q in range(V_PARTS)]
        ch_out, v_out = [], []
        for j, (off, width) in enumerate(ch_order):
            ch_out.append(_dot(tied(xe, j == 0), w_in_ref[:, off:off + width]))
            if j < V_PARTS:
                off, width = v_parts[j]
                v_out.append(jax.nn.gelu(_dot(tied(xm, j == 0), w_in_ref[:, off:off + width])))
        assert next(tok, None) is None
        for p in range(CH_PARTS):
            width = w_a // CH_PARTS
            ch_ref[:, p * width:(p + 1) * width] = ch_out[2 * p] * ch_out[2 * p + 1]
        v = _layernorm(jnp.concatenate(v_out, axis=1), vg_ref[...], vb_ref[...])
        v = v.astype(jnp.bfloat16)
        u = jax.nn.gelu(_dot(xm, w_in_ref[:, off_ub:off_vb]))
        b_a = _dot(xm, w_in_ref[:, 0:off_ca])

        head_cols = []
        for h in range(n_heads):
            vh = jnp.concatenate(
                [v[k * CHUNK:(k + 1) * CHUNK, h * DH_B:(h + 1) * DH_B] for k in range(n_chunks)],
                axis=1)
            mh = _dot(ws_ref[h], vh)
            mh = mh + jnp.concatenate([bs_ref[:, h * DH_B:(h + 1) * DH_B]] * n_chunks, axis=1)
            head_cols.append(jnp.concatenate(
                [mh[:, k * DH_B:(k + 1) * DH_B] for k in range(n_chunks)], axis=0))
        mixed = jnp.concatenate(head_cols, axis=1)

        cw = convw_ref[...]
        conv = (ch_ref[pl.ds(HALO - 1, ts), :] * cw[0:1, :]
                + ch_ref[pl.ds(HALO, ts), :] * cw[1:2, :]
                + ch_ref[pl.ds(HALO + 1, ts), :] * cw[2:3, :])
        a = (b_a * conv).astype(jnp.bfloat16)
        g_b = _sigmoid(_dot(xm, w_in_ref[:, off_gb:off_gb + d_model])
                       + bgate_ref[:, d_model:2 * d_model])
        y_b =_dot((u * mixed).astype(jnp.bfloat16), wpb_ref[...])
        g_a = _sigmoid(_dot(xm, w_in_ref[:, off_ga:off_gb]) + bgate_ref[:, 0:d_model])
        acc = g_b * y_b
        y_a = _dot(a, wpa_ref[...])
        acc = acc + g_a * y_a
        carry_ref[...] = alpha * x + _dot(acc.astype(jnp.bfloat16), wo_ref[...])

    _deferred_norm_steps(carry_ref, o_ref, g1_ref, b1_ref, NORM_BLOCKS_MIX, tile_body,
                         setup=stage_weights)


def _ffn_kernel(x_ref, w1_ref, w2_ref, g2_ref, b2_ref, o_ref, carry_ref, *, alpha):
    n_ff = w1_ref.shape[1] // FF_CHUNK
    assert NORM_BLOCKS_FFN == 2 * n_ff

    def tile_body(tokens):
        x = x_ref[...]
        xb = x.astype(jnp.bfloat16)
        y = alpha * x
        for k in range(n_ff):
            lhs = xb if k == 0 else _tie(xb, tokens[2 * k - 1])
            hid = jnp.maximum(_dot(lhs, w1_ref[:, k * FF_CHUNK:(k + 1) * FF_CHUNK]), 0.0)
            hid = _tie((hid * hid).astype(jnp.bfloat16), tokens[2 * k])
            if k == n_ff - 1:
                hid = _tie(hid, tokens[2 * k + 1])
            y = y + _dot(hid, w2_ref[k * FF_CHUNK:(k + 1) * FF_CHUNK, :])
        carry_ref[...] = y

    _deferred_norm_steps(carry_ref, o_ref, g2_ref, b2_ref, NORM_BLOCKS_FFN, tile_body)


def _resident(shape):
    nd = len(shape)
    return pl.BlockSpec(shape, lambda i: (0,) * nd, pipeline_mode=pl.Buffered(1))


def _mix_layer(x2d, w_in, b_gate, conv_w, vg, vb, w_s, bs_full, w_pa, w_pb, w_o, g1, b1,
               w_ff1, w_ff2, *, alpha):
    s, d = x2d.shape
    w_a = w_pa.shape[0]
    w_b = w_pb.shape[0]
    d_ff = w_ff1.shape[1]
    ts = SEQ_TILE_MIX
    assert s % ts == 0 and ts % CHUNK == 0
    n_tiles = s // ts
    hb = ts // HALO
    n_halo_blocks = s // HALO
    r1, r2 = d // n_tiles, d_ff // n_tiles
    assert r1 * n_tiles == d and r2 * n_tiles == d_ff
    assert r1 % BF16_ROWS == 0 and r2 % BF16_ROWS == 0
    last = n_tiles - 1
    tile = lambda i: jnp.minimum(i, last)
    kern = functools.partial(_mix_kernel, alpha=alpha, w_a=w_a, w_b=w_b, d_model=d)
    in_hbm = pl.BlockSpec(memory_space=pl.ANY)
    return pl.pallas_call(
        kern,
        grid=(n_tiles + 1,),
        in_specs=[
            pl.BlockSpec((HALO, d), lambda i: (jnp.maximum(tile(i) * hb - 1, 0), 0)),
            pl.BlockSpec((ts, d), lambda i: (tile(i), 0)),
            pl.BlockSpec((HALO, d),
                         lambda i: (jnp.minimum((tile(i) + 1) * hb, n_halo_blocks - 1), 0)),
            in_hbm, _resident(b_gate.shape), _resident(conv_w.shape),
            _resident(vg.shape), _resident(vb.shape), _resident(w_s.shape),
            _resident(bs_full.shape), in_hbm, in_hbm, in_hbm,
            _resident(g1.shape), _resident(b1.shape),
            pl.BlockSpec((r1, d_ff), lambda i: (tile(i), 0)),
            pl.BlockSpec((r2, d), lambda i: (tile(i), 0)),
        ],
        out_specs=[
            pl.BlockSpec((ts, d), lambda i: (jnp.maximum(i - 1, 0), 0)),
            pl.BlockSpec((r1, d_ff), lambda i: (tile(i), 0)),
            pl.BlockSpec((r2, d), lambda i: (tile(i), 0)),
        ],
        out_shape=[
            jax.ShapeDtypeStruct((s, d), jnp.float32),
            jax.ShapeDtypeStruct(w_ff1.shape, jnp.bfloat16),
            jax.ShapeDtypeStruct(w_ff2.shape, jnp.bfloat16),
        ],
        scratch_shapes=[
            pltpu.VMEM(w_in.shape, jnp.bfloat16),
            pltpu.VMEM(w_pa.shape, jnp.bfloat16),
            pltpu.VMEM(w_pb.shape, jnp.bfloat16),
            pltpu.VMEM(w_o.shape, jnp.bfloat16),
            pltpu.VMEM((ts + 2 * HALO, d), jnp.bfloat16),
            pltpu.VMEM((ts + 2 * HALO, w_a), jnp.float32),
            pltpu.VMEM((ts, d), jnp.float32),
        ],
        compiler_params=pltpu.CompilerParams(
            dimension_semantics=("arbitrary",), vmem_limit_bytes=VMEM_LIMIT_BYTES),
        name="token_mix",
    )(x2d, x2d, x2d, w_in, b_gate, conv_w, vg, vb, w_s, bs_full, w_pa, w_pb, w_o, g1, b1,
      w_ff1, w_ff2)


def _ffn_layer(x2d, w1, w2, g2, b2, *, alpha):
    s, d = x2d.shape
    ts = SEQ_TILE_FFN
    assert s % ts == 0
    n_tiles = s // ts
    return pl.pallas_call(
        functools.partial(_ffn_kernel, alpha=alpha),
        grid=(n_tiles + 1,),
        in_specs=[
            pl.BlockSpec((ts, d), lambda i: (jnp.minimum(i, n_tiles - 1), 0)),
            _resident(w1.shape), _resident(w2.shape), _resident(g2.shape), _resident(b2.shape),
        ],
        out_specs=pl.BlockSpec((ts, d), lambda i: (jnp.maximum(i - 1, 0), 0)),
        out_shape=jax.ShapeDtypeStruct((s, d), jnp.float32),
        scratch_shapes=[pltpu.VMEM((ts, d), jnp.float32)],
        compiler_params=pltpu.CompilerParams(
            dimension_semantics=("arbitrary",), vmem_limit_bytes=VMEM_LIMIT_BYTES),
        name="channel_mix",
    )(x2d, w1, w2, g2, b2)


def kernel(x, w_in, b_gate, conv_w, v_norm_g, v_norm_b, w_s, b_s, w_pa, w_pb, w_o,
           ln1_g, ln1_b, w_ff1, w_ff2, ln2_g, ln2_b):
    bsz, seq, d = x.shape
    depth = w_in.shape[0]
    alpha = (2.0 * depth) ** 0.25
    bf = jnp.bfloat16
    row = lambda p: p.reshape(1, -1)
    outs = []
    for b in range(bsz):
        xb = x[b]
        for l in range(depth):
            bs_full = jnp.repeat(jnp.transpose(b_s[l]), DH_B, axis=1)
            xb, w1b, w2b = _mix_layer(
                xb, w_in[l], row(b_gate[l]), conv_w[l], row(v_norm_g[l]),
                row(v_norm_b[l]), w_s[l].astype(bf), bs_full, w_pa[l], w_pb[l], w_o[l],
                row(ln1_g[l]), row(ln1_b[l]),
                w_ff1[l], w_ff2[l], alpha=alpha)
            xb = _ffn_layer(xb, w1b, w2b, row(ln2_g[l]), row(ln2_b[l]), alpha=alpha)
        outs.append(xb)
    return jnp.stack(outs, axis=0) if bsz > 1 else outs[0][None]
```

```python
import functools
import math

import jax
import jax.numpy as jnp
from jax import lax
from jax.experimental import pallas as pl
from jax.experimental.pallas import tpu as pltpu

CONV_K = 3
CHUNK = 128
DH_B = 128
LN_EPS = 1e-5
GELU_C1 = math.sqrt(2.0 / math.pi)
GELU_C2 = GELU_C1 * 0.044715

LANES = 128
BF16_ROWS = 16
HALO = BF16_ROWS
VMEM_LIMIT_BYTES = 60 * 1024 * 1024

SEQ_TILE_MIX = 512
SEQ_TILE_FFN = 512
FF_CHUNK = 1024
CH_PARTS = 3
V_PARTS = 4
NORM_BLOCKS_MIX = 2 * CH_PARTS + V_PARTS - 2
NORM_BLOCKS_FFN = 8
STAGE_SLOTS = 4
STAGE_ROWS_W_IN = 32
STAGE_ROWS_PROJ = 256


def _layernorm(y, g, b, eps=LN_EPS):
    mu = jnp.mean(y, axis=-1, keepdims=True)
    yc = y - mu
    var = jnp.mean(yc * yc, axis=-1, keepdims=True)
    return yc * lax.rsqrt(var + eps) * g + b


def _twice_sigmoid(half_z):
    return 1.0 + jnp.tanh(half_z)


def _twice_gelu(x):
    t = jnp.tanh(x * (GELU_C1 + GELU_C2 * (x * x)))
    return x + x * t


def _dot(a, b):
    return jnp.dot(a, b, preferred_element_type=jnp.float32)


def _token(y):
    rows, cols = y.shape
    assert rows % BF16_ROWS == 0 and cols % LANES == 0
    t = y.reshape(rows // BF16_ROWS, BF16_ROWS, cols).sum(axis=0)
    t = functools.reduce(
        lambda p, q: p + q, [t[:, c * LANES:(c + 1) * LANES] for c in range(cols // LANES)])
    return t.astype(jnp.bfloat16)


def _norm_carried_tile(carry_ref, o_ref, g_ref, b_ref, n_blocks):
    rows = carry_ref.shape[0] // n_blocks
    assert rows * n_blocks == carry_ref.shape[0]
    tokens = []
    for r in range(n_blocks):
        y = _layernorm(carry_ref[pl.ds(r * rows, rows), :], g_ref[...], b_ref[...])
        o_ref[pl.ds(r * rows, rows), :] = y
        tokens.append(_token(y))
    return tokens


def _tie(value, token):
    never = pl.program_id(0) < 0
    r, c = token.shape
    head = jnp.where(never, token, value[0:r, 0:c])
    top = jnp.concatenate([head, value[0:r, c:]], axis=1)
    return jnp.concatenate([top, value[r:]], axis=0)


def _stage_weights_bf16(families):
    lookahead = STAGE_SLOTS - 1

    def staged(*scratch):
        plans = []
        for f, (pairs, rows) in enumerate(families):
            stage, sem = scratch[2 * f], scratch[2 * f + 1]
            chunks = []
            for src, dst, col_scales in pairs:
                assert src.shape == dst.shape and src.shape[0] % rows == 0
                for r0 in range(0, src.shape[0], rows):
                    k = len(chunks)
                    copy = pltpu.make_async_copy(
                        src.at[pl.ds(r0, rows), :], stage.at[k % STAGE_SLOTS],
                        sem.at[k % STAGE_SLOTS])
                    chunks.append((copy, stage.at[k % STAGE_SLOTS],
                                   dst.at[pl.ds(r0, rows), :], col_scales))
            plans.append(chunks)
        for chunks in plans:
            for k in range(min(lookahead, len(chunks))):
                chunks[k][0].start(priority=k % 2)
        for chunks in plans:
            for k, (copy, stage_slot, dst_rows, col_scales) in enumerate(chunks):
                if k + lookahead < len(chunks):
                    chunks[k + lookahead][0].start(priority=(k + lookahead) % 2)
                copy.wait()
                for c0, c1, scale in col_scales:
                    block = stage_slot[:, c0:c1]
                    if scale != 1.0:
                        block = block * scale
                    dst_rows[:, c0:c1] = block.astype(jnp.bfloat16)

    scratch_types = []
    for pairs, rows in families:
        cols = pairs[0][0].shape[1]
        scratch_types += [pltpu.VMEM((STAGE_SLOTS, rows, cols), jnp.float32),
                          pltpu.SemaphoreType.DMA((STAGE_SLOTS,))]
    pl.run_scoped(staged, *scratch_types)


def _deferred_norm_steps(carry_ref, o_ref, g_ref, b_ref, n_blocks, tile_body, setup=None):
    i = pl.program_id(0)
    n_tiles = pl.num_programs(0) - 1

    @pl.when(i == 0)
    def _():
        carry_ref[...] = jnp.zeros_like(carry_ref)
        if setup is not None:
            setup()

    @pl.when(i < n_tiles)
    def _():
        tile_body(_norm_carried_tile(carry_ref, o_ref, g_ref, b_ref, n_blocks))

    @pl.when(i == n_tiles)
    def _():
        _norm_carried_tile(carry_ref, o_ref, g_ref, b_ref, n_blocks)


def _mix_kernel(x_ref, xnext_ref, w_in_hbm, bgate_ref, convw_ref,
                vg_ref, vb_ref, ws_ref, bs_ref, wpa_hbm, wpb_hbm, wo_hbm,
                g1_ref, b1_ref, w1f_ref, w2f_ref, o_ref, w1b_ref, w2b_ref,
                w_in_ref, wpa_ref, wpb_ref, wo_ref, xe_ref, ch_ref, carry_ref,
                *, alpha, w_a, w_b, d_model):
    ts = x_ref.shape[0]

    i = pl.program_id(0)
    n_tiles = pl.num_programs(0) - 1
    off_ca = w_a
    off_ha = 2 * w_a
    off_ub = 3 * w_a
    off_vb = off_ub + w_b
    off_ga = off_vb + w_b
    off_gb = off_ga + d_model
    n_proj = off_gb + d_model

    def first_step_setup():
        ch_ref[pl.ds(ts, HALO), :] = jnp.zeros((HALO, ch_ref.shape[1]), jnp.float32)
        whole = lambda ref, scale: [(0, ref.shape[1], scale)]
        _stage_weights_bf16([
            ([(w_in_hbm, w_in_ref, [(0, off_ga, 1.0), (off_ga, n_proj, 0.5)])], STAGE_ROWS_W_IN),
            ([(wpa_hbm, wpa_ref, whole(wpa_ref, 1.0)), (wpb_hbm, wpb_ref, whole(wpb_ref, 0.5)),
              (wo_hbm, wo_ref, whole(wo_ref, 0.5))], STAGE_ROWS_PROJ),
        ])

    def tile_body(tokens):
        w1b_ref[...] = w1f_ref[...].astype(jnp.bfloat16)
        w2b_ref[...] = w2f_ref[...].astype(jnp.bfloat16)

        x = x_ref[...]
        xe_ref[pl.ds(0, ts), :] = x.astype(jnp.bfloat16)
        xe_ref[pl.ds(ts, HALO), :] = jnp.where(
            i < n_tiles - 1, xnext_ref[...], 0.0).astype(jnp.bfloat16)
        xe = xe_ref[...]
        xm = xe_ref[pl.ds(0, ts), :]
        ch_ref[pl.ds(0, HALO), :] = ch_ref[pl.ds(ts, HALO), :]
        n_chunks = ts // CHUNK
        n_heads = w_b // DH_B

        tok = iter(tokens)
        tied = lambda lhs, first: lhs if first else _tie(lhs, next(tok))
        c_parts = [(off_ca + p * (w_a // CH_PARTS), w_a // CH_PARTS) for p in range(CH_PARTS)]
        h_parts = [(off_ha + p * (w_a // CH_PARTS), w_a // CH_PARTS) for p in range(CH_PARTS)]
        ch_order = [q for pair in zip(c_parts, h_parts) for q in pair]
        v_parts = [(off_vb + q * (w_b // V_PARTS), w_b // V_PARTS) for q in range(V_PARTS)]
        ch_out, v_out = [], []
        for j, (off, width) in enumerate(ch_order):
            ch_out.append(_dot(tied(xe, j == 0), w_in_ref[:, off:off + width]))
            if j < V_PARTS:
                off, width = v_parts[j]
                v_out.append(_twice_gelu(_dot(tied(xm, j == 0), w_in_ref[:, off:off + width])))
        assert next(tok, None) is None
        for p in range(CH_PARTS):
            width = w_a // CH_PARTS
            ch_ref[pl.ds(HALO, ts + HALO), p * width:(p + 1) * width] = (
                ch_out[2 * p] * ch_out[2 * p + 1])
        v = _layernorm(jnp.concatenate(v_out, axis=1), vg_ref[...], vb_ref[...], eps=4.0 * LN_EPS)
        v = v.astype(jnp.bfloat16)

        cw = convw_ref[...]
        a_parts = []
        for p in range(CH_PARTS):
            cols = slice(p * (w_a // CH_PARTS), (p + 1) * (w_a // CH_PARTS))
            conv = (ch_ref[pl.ds(HALO - 1, ts), cols] * cw[0:1, cols]
                    + ch_ref[pl.ds(HALO, ts), cols] * cw[1:2, cols]
                    + ch_ref[pl.ds(HALO + 1, ts), cols] * cw[2:3, cols])
            b_p = _dot(xm, w_in_ref[:, cols])
            a_parts.append((b_p * conv).astype(jnp.bfloat16))
        a = jnp.concatenate(a_parts, axis=1)
        u2 = _twice_gelu(_dot(xm, w_in_ref[:, off_ub:off_vb]))

        head_cols = []
        for h in range(n_heads):
            vh = jnp.concatenate(
                [v[k * CHUNK:(k + 1) * CHUNK, h * DH_B:(h + 1) * DH_B] for k in range(n_chunks)],
                axis=1)
            mh = _dot(ws_ref[h], vh)
            mh = mh + jnp.concatenate([bs_ref[:, h * DH_B:(h + 1) * DH_B]] * n_chunks, axis=1)
            head_cols.append(jnp.concatenate(
                [mh[:, k * DH_B:(k + 1) * DH_B] for k in range(n_chunks)], axis=0))
        mixed = jnp.concatenate(head_cols, axis=1)

        half_bgate = 0.5 * bgate_ref[...]
        g2_b = _twice_sigmoid(_dot(xm, w_in_ref[:, off_gb:n_proj]) + half_bgate[:, d_model:])
        y_b = _dot((u2 * mixed).astype(jnp.bfloat16), wpb_ref[...])
        g2_a = _twice_sigmoid(_dot(xm, w_in_ref[:, off_ga:off_gb]) + half_bgate[:, :d_model])
        acc2 = g2_b * y_b
        y_a = _dot(a, wpa_ref[...])
        acc2 = acc2 + g2_a * y_a
        carry_ref[...] = alpha * x + _dot(acc2.astype(jnp.bfloat16), wo_ref[...])

    _deferred_norm_steps(carry_ref, o_ref, g1_ref, b1_ref, NORM_BLOCKS_MIX, tile_body,
                         setup=first_step_setup)


def _ffn_kernel(x_ref, w1_ref, w2_ref, g2_ref, b2_ref, o_ref, carry_ref, *, alpha):
    n_ff = w1_ref.shape[1] // FF_CHUNK
    assert NORM_BLOCKS_FFN == 2 * n_ff

    def tile_body(tokens):
        x = x_ref[...]
        xb = x.astype(jnp.bfloat16)
        y = alpha * x
        for k in range(n_ff):
            lhs = xb if k == 0 else _tie(xb, tokens[2 * k - 1])
            hid = jnp.maximum(_dot(lhs, w1_ref[:, k * FF_CHUNK:(k + 1) * FF_CHUNK]), 0.0)
            hid = _tie((hid * hid).astype(jnp.bfloat16), tokens[2 * k])
            if k == n_ff - 1:
                hid = _tie(hid, tokens[2 * k + 1])
            y = y + _dot(hid, w2_ref[k * FF_CHUNK:(k + 1) * FF_CHUNK, :])
        carry_ref[...] = y

    _deferred_norm_steps(carry_ref, o_ref, g2_ref, b2_ref, NORM_BLOCKS_FFN, tile_body)


def _resident(shape):
    nd = len(shape)
    return pl.BlockSpec(shape, lambda i: (0,) * nd, pipeline_mode=pl.Buffered(1))


def _mix_layer(x2d, w_in, b_gate, conv_w, vg, vb, w_s, bs_full, w_pa, w_pb, w_o, g1, b1,
               w_ff1, w_ff2, *, alpha):
    s, d = x2d.shape
    w_a = w_pa.shape[0]
    w_b = w_pb.shape[0]
    d_ff = w_ff1.shape[1]
    ts = SEQ_TILE_MIX
    assert s % ts == 0 and ts % CHUNK == 0
    n_tiles = s // ts
    hb = ts // HALO
    n_halo_blocks = s // HALO
    assert ts % HALO == 0
    r1, r2 = d // n_tiles, d_ff // n_tiles
    assert r1 * n_tiles == d and r2 * n_tiles == d_ff
    assert r1 % BF16_ROWS == 0 and r2 % BF16_ROWS == 0
    last = n_tiles - 1
    tile = lambda i: jnp.minimum(i, last)
    kern = functools.partial(_mix_kernel, alpha=alpha, w_a=w_a, w_b=w_b, d_model=d)
    in_hbm = pl.BlockSpec(memory_space=pl.ANY)
    return pl.pallas_call(
        kern,
        grid=(n_tiles + 1,),
        in_specs=[
            pl.BlockSpec((ts, d), lambda i: (tile(i), 0)),
            pl.BlockSpec((HALO, d),
                         lambda i: (jnp.minimum((tile(i) + 1) * hb, n_halo_blocks - 1), 0)),
            in_hbm, _resident(b_gate.shape), _resident(conv_w.shape),
            _resident(vg.shape), _resident(vb.shape), _resident(w_s.shape),
            _resident(bs_full.shape), in_hbm, in_hbm, in_hbm,
            _resident(g1.shape), _resident(b1.shape),
            pl.BlockSpec((r1, d_ff), lambda i: (tile(i), 0)),
            pl.BlockSpec((r2, d), lambda i: (tile(i), 0)),
        ],
        out_specs=[
            pl.BlockSpec((ts, d), lambda i: (jnp.maximum(i - 1, 0), 0)),
            pl.BlockSpec((r1, d_ff), lambda i: (tile(i), 0)),
            pl.BlockSpec((r2, d), lambda i: (tile(i), 0)),
        ],
        out_shape=[
            jax.ShapeDtypeStruct((s, d), jnp.float32),
            jax.ShapeDtypeStruct(w_ff1.shape, jnp.bfloat16),
            jax.ShapeDtypeStruct(w_ff2.shape, jnp.bfloat16),
        ],
        scratch_shapes=[
            pltpu.VMEM(w_in.shape, jnp.bfloat16),
            pltpu.VMEM(w_pa.shape, jnp.bfloat16),
            pltpu.VMEM(w_pb.shape, jnp.bfloat16),
            pltpu.VMEM(w_o.shape, jnp.bfloat16),
            pltpu.VMEM((ts + HALO, d), jnp.bfloat16),
            pltpu.VMEM((ts + 2 * HALO, w_a), jnp.float32),
            pltpu.VMEM((ts, d), jnp.float32),
        ],
        compiler_params=pltpu.CompilerParams(
            dimension_semantics=("arbitrary",), vmem_limit_bytes=VMEM_LIMIT_BYTES),
        name="token_mix",
    )(x2d, x2d, w_in, b_gate, conv_w, vg, vb, w_s, bs_full, w_pa, w_pb, w_o, g1, b1,
      w_ff1, w_ff2)


def _ffn_layer(x2d, w1, w2, g2, b2, *, alpha):
    s, d = x2d.shape
    ts = SEQ_TILE_FFN
    assert s % ts == 0
    n_tiles = s // ts
    return pl.pallas_call(
        functools.partial(_ffn_kernel, alpha=alpha),
        grid=(n_tiles + 1,),
        in_specs=[
            pl.BlockSpec((ts, d), lambda i: (jnp.minimum(i, n_tiles - 1), 0)),
            _resident(w1.shape), _resident(w2.shape), _resident(g2.shape), _resident(b2.shape),
        ],
        out_specs=pl.BlockSpec((ts, d), lambda i: (jnp.maximum(i - 1, 0), 0)),
        out_shape=jax.ShapeDtypeStruct((s, d), jnp.float32),
        scratch_shapes=[pltpu.VMEM((ts, d), jnp.float32)],
        compiler_params=pltpu.CompilerParams(
            dimension_semantics=("arbitrary",), vmem_limit_bytes=VMEM_LIMIT_BYTES),
        name="channel_mix",
    )(x2d, w1, w2, g2, b2)


def kernel(x, w_in, b_gate, conv_w, v_norm_g, v_norm_b, w_s, b_s, w_pa, w_pb, w_o,
           ln1_g, ln1_b, w_ff1, w_ff2, ln2_g, ln2_b):
    bsz, seq, d = x.shape
    depth = w_in.shape[0]
    alpha = (2.0 * depth) ** 0.25
    bf = jnp.bfloat16
    row = lambda p: p.reshape(1, -1)
    outs = []
    for b in range(bsz):
        xb = x[b]
        for l in range(depth):
            bs_full = jnp.repeat(jnp.transpose(b_s[l]), DH_B, axis=1)
            xb, w1b, w2b = _mix_layer(
                xb, w_in[l], row(b_gate[l]), conv_w[l], row(v_norm_g[l]),
                row(v_norm_b[l]), w_s[l].astype(bf), bs_full, w_pa[l], w_pb[l], w_o[l],
                row(ln1_g[l]), row(ln1_b[l]),
                w_ff1[l], w_ff2[l], alpha=alpha)
            xb = _ffn_layer(xb, w1b, w2b, row(ln2_g[l]), row(ln2_b[l]), alpha=alpha)
        outs.append(xb)
    return jnp.stack(outs, axis=0) if bsz > 1 else outs[0][None]
```

```python
import functools
import math

import jax
import jax.numpy as jnp
from jax import lax
from jax.experimental import pallas as pl
from jax.experimental.pallas import tpu as pltpu

CONV_K = 3
CHUNK = 128
DH_B = 128
LN_EPS = 1e-5
GELU_C1 = math.sqrt(2.0 / math.pi)
GELU_C2 = GELU_C1 * 0.044715

LANES = 128
BF16_ROWS = 16
HALO = BF16_ROWS
VMEM_LIMIT_BYTES = 60 * 1024 * 1024

SEQ_TILE_MIX = 512
SEQ_TILE_FFN = 1024
FF_CHUNK = 1024
CH_PARTS = 3
V_PARTS = 4
NORM_BLOCKS_MIX = 2 * CH_PARTS + V_PARTS - 2
NORM_BLOCKS_FFN = 8
STAGE_SLOTS = 4
STAGE_ROWS_W_IN = 32
STAGE_ROWS_PROJ = 256


def _layernorm(y, g, b, eps=LN_EPS):
    mu = jnp.mean(y, axis=-1, keepdims=True)
    yc = y - mu
    var = jnp.mean(yc * yc, axis=-1, keepdims=True)
    return yc * lax.rsqrt(var + eps) * g + b


def _twice_sigmoid(half_z):
    return 1.0 + jnp.tanh(half_z)


def _twice_gelu(x):
    t = jnp.tanh(x * (GELU_C1 + GELU_C2 * (x * x)))
    return x + x * t


def _dot(a, b):
    return jnp.dot(a, b, preferred_element_type=jnp.float32)


def _token(y):
    rows, cols = y.shape
    assert rows % BF16_ROWS == 0 and cols % LANES == 0
    t = y.reshape(rows // BF16_ROWS, BF16_ROWS, cols).sum(axis=0)
    t = functools.reduce(
        lambda p, q: p + q, [t[:, c * LANES:(c + 1) * LANES] for c in range(cols // LANES)])
    return t.astype(jnp.bfloat16)


def _norm_carried_tile(carry_ref, o_ref, g_ref, b_ref, n_blocks):
    rows = carry_ref.shape[0] // n_blocks
    assert rows * n_blocks == carry_ref.shape[0]
    tokens = []
    for r in range(n_blocks):
        y = _layernorm(carry_ref[pl.ds(r * rows, rows), :], g_ref[...], b_ref[...])
        o_ref[pl.ds(r * rows, rows), :] = y
        tokens.append(_token(y))
    return tokens


def _tie(value, token):
    never = pl.program_id(0) < 0
    r, c = token.shape
    head = jnp.where(never, token, value[0:r, 0:c])
    top = jnp.concatenate([head, value[0:r, c:]], axis=1)
    return jnp.concatenate([top, value[r:]], axis=0)


def _stage_weights_bf16(families):
    lookahead = STAGE_SLOTS - 1

    def staged(*scratch):
        plans = []
        for f, (pairs, rows) in enumerate(families):
            stage, sem = scratch[2 * f], scratch[2 * f + 1]
            chunks = []
            for src, dst, col_scales in pairs:
                assert src.shape == dst.shape and src.shape[0] % rows == 0
                for r0 in range(0, src.shape[0], rows):
                    k = len(chunks)
                    copy = pltpu.make_async_copy(
                        src.at[pl.ds(r0, rows), :], stage.at[k % STAGE_SLOTS],
                        sem.at[k % STAGE_SLOTS])
                    chunks.append((copy, stage.at[k % STAGE_SLOTS],
                                   dst.at[pl.ds(r0, rows), :], col_scales))
            plans.append(chunks)
        for chunks in plans:
            for k in range(min(lookahead, len(chunks))):
                chunks[k][0].start(priority=k % 2)
        for chunks in plans:
            for k, (copy, stage_slot, dst_rows, col_scales) in enumerate(chunks):
                if k + lookahead < len(chunks):
                    chunks[k + lookahead][0].start(priority=(k + lookahead) % 2)
                copy.wait()
                for c0, c1, scale in col_scales:
                    block = stage_slot[:, c0:c1]
                    if scale != 1.0:
                        block = block * scale
                    dst_rows[:, c0:c1] = block.astype(jnp.bfloat16)

    scratch_types = []
    for pairs, rows in families:
        cols = pairs[0][0].shape[1]
        scratch_types += [pltpu.VMEM((STAGE_SLOTS, rows, cols), jnp.float32),
                          pltpu.SemaphoreType.DMA((STAGE_SLOTS,))]
    pl.run_scoped(staged, *scratch_types)


def _deferred_norm_steps(carry_ref, o_ref, g_ref, b_ref, n_blocks, tile_body, setup=None):
    i = pl.program_id(0)
    n_tiles = pl.num_programs(0) - 1

    @pl.when(i == 0)
    def _():
        carry_ref[...] = jnp.zeros_like(carry_ref)
        if setup is not None:
            setup()

    @pl.when(i < n_tiles)
    def _():
        tile_body(_norm_carried_tile(carry_ref, o_ref, g_ref, b_ref, n_blocks))

    @pl.when(i == n_tiles)
    def _():
        _norm_carried_tile(carry_ref, o_ref, g_ref, b_ref, n_blocks)


def _mix_kernel(x_ref, xnext_ref, w_in_hbm, bgate_ref, convw_ref,
                vg_ref, vb_ref, ws_ref, bs_ref, wpa_hbm, wpb_hbm, wo_hbm,
                g1_ref, b1_ref, w1f_ref, w2f_ref, o_ref, w1b_ref, w2b_ref,
                w_in_ref, wpa_ref, wpb_ref, wo_ref, xe_ref, ch_ref, carry_ref,
                *, alpha, w_a, w_b, d_model):
    ts = x_ref.shape[0]

    i = pl.program_id(0)
    n_tiles = pl.num_programs(0) - 1
    off_ca = w_a
    off_ha = 2 * w_a
    off_ub = 3 * w_a
    off_vb = off_ub + w_b
    off_ga = off_vb + w_b
    off_gb = off_ga + d_model
    n_proj = off_gb + d_model

    def first_step_setup():
        ch_ref[pl.ds(ts, HALO), :] = jnp.zeros((HALO, ch_ref.shape[1]), jnp.float32)
        whole = lambda ref, scale: [(0, ref.shape[1], scale)]
        _stage_weights_bf16([
            ([(w_in_hbm, w_in_ref, [(0, off_ga, 1.0), (off_ga, n_proj, 0.5)])], STAGE_ROWS_W_IN),
            ([(wpa_hbm, wpa_ref, whole(wpa_ref, 1.0)), (wpb_hbm, wpb_ref, whole(wpb_ref, 0.5)),
              (wo_hbm, wo_ref, whole(wo_ref, 0.5))], STAGE_ROWS_PROJ),
        ])

    def tile_body(tokens):
        w1b_ref[...] = w1f_ref[...].astype(jnp.bfloat16)
        w2b_ref[...] = w2f_ref[...].astype(jnp.bfloat16)

        x = x_ref[...]
        xe_ref[pl.ds(0, ts), :] = x.astype(jnp.bfloat16)
        xe_ref[pl.ds(ts, HALO), :] = jnp.where(
            i < n_tiles - 1, xnext_ref[...], 0.0).astype(jnp.bfloat16)
        xe = xe_ref[...]
        xm = xe_ref[pl.ds(0, ts), :]
        ch_ref[pl.ds(0, HALO), :] = ch_ref[pl.ds(ts, HALO), :]
        n_chunks = ts // CHUNK
        n_heads = w_b // DH_B

        tok = iter(tokens)
        tied = lambda lhs, first: lhs if first else _tie(lhs, next(tok))
        c_parts = [(off_ca + p * (w_a // CH_PARTS), w_a // CH_PARTS) for p in range(CH_PARTS)]
        h_parts = [(off_ha + p * (w_a // CH_PARTS), w_a // CH_PARTS) for p in range(CH_PARTS)]
        ch_order = [q for pair in zip(c_parts, h_parts) for q in pair]
        v_parts = [(off_vb + q * (w_b // V_PARTS), w_b // V_PARTS) for q in range(V_PARTS)]
        ch_out, v_out = [], []
        for j, (off, width) in enumerate(ch_order):
            ch_out.append(_dot(tied(xe, j == 0), w_in_ref[:, off:off + width]))
            if j < V_PARTS:
                off, width = v_parts[j]
                v_out.append(_twice_gelu(_dot(tied(xm, j == 0), w_in_ref[:, off:off + width])))
        assert next(tok, None) is None
        for p in range(CH_PARTS):
            width = w_a // CH_PARTS
            ch_ref[pl.ds(HALO, ts + HALO), p * width:(p + 1) * width] = (
                ch_out[2 * p] * ch_out[2 * p + 1])
        v = _layernorm(jnp.concatenate(v_out, axis=1), vg_ref[...], vb_ref[...], eps=4.0 * LN_EPS)
        v = v.astype(jnp.bfloat16)

        cw = convw_ref[...]
        a_parts = []
        for p in range(CH_PARTS):
            cols = slice(p * (w_a // CH_PARTS), (p + 1) * (w_a // CH_PARTS))
            conv = (ch_ref[pl.ds(HALO - 1, ts), cols] * cw[0:1, cols]
                    + ch_ref[pl.ds(HALO, ts), cols] * cw[1:2, cols]
                    + ch_ref[pl.ds(HALO + 1, ts), cols] * cw[2:3, cols])
            b_p = _dot(xm, w_in_ref[:, cols])
            a_parts.append((b_p * conv).astype(jnp.bfloat16))
        a = jnp.concatenate(a_parts, axis=1)
        u2 = _twice_gelu(_dot(xm, w_in_ref[:, off_ub:off_vb]))

        head_cols = []
        for h in range(n_heads):
            vh = jnp.concatenate(
                [v[k * CHUNK:(k + 1) * CHUNK, h * DH_B:(h + 1) * DH_B] for k in range(n_chunks)],
                axis=1)
            mh = _dot(ws_ref[h], vh)
            mh = mh + jnp.concatenate([bs_ref[:, h * DH_B:(h + 1) * DH_B]] * n_chunks, axis=1)
            head_cols.append(jnp.concatenate(
                [mh[:, k * DH_B:(k + 1) * DH_B] for k in range(n_chunks)], axis=0))
        mixed = jnp.concatenate(head_cols, axis=1)

        half_bgate = 0.5 * bgate_ref[...]
        g2_b = _twice_sigmoid(_dot(xm, w_in_ref[:, off_gb:n_proj]) + half_bgate[:, d_model:])
        y_b = _dot((u2 * mixed).astype(jnp.bfloat16), wpb_ref[...])
        g2_a = _twice_sigmoid(_dot(xm, w_in_ref[:, off_ga:off_gb]) + half_bgate[:, :d_model])
        acc2 = g2_b * y_b
        y_a = _dot(a, wpa_ref[...])
        acc2 = acc2 + g2_a * y_a
        carry_ref[...] = alpha * x + _dot(acc2.astype(jnp.bfloat16), wo_ref[...])

    _deferred_norm_steps(carry_ref, o_ref, g1_ref, b1_ref, NORM_BLOCKS_MIX, tile_body,
                         setup=first_step_setup)


def _ffn_kernel(x_ref, w1_ref, w2_ref, g2_ref, b2_ref, o_ref, carry_ref, *, alpha):
    n_ff = w1_ref.shape[1] // FF_CHUNK
    assert NORM_BLOCKS_FFN == 2 * n_ff

    def tile_body(tokens):
        x = x_ref[...]
        xb = x.astype(jnp.bfloat16)
        y = alpha * x
        for k in range(n_ff):
            lhs = xb if k == 0 else _tie(xb, tokens[2 * k - 1])
            hid = jnp.maximum(_dot(lhs, w1_ref[:, k * FF_CHUNK:(k + 1) * FF_CHUNK]), 0.0)
            hid = _tie((hid * hid).astype(jnp.bfloat16), tokens[2 * k])
            if k == n_ff - 1:
                hid = _tie(hid, tokens[2 * k + 1])
            y = y + _dot(hid, w2_ref[k * FF_CHUNK:(k + 1) * FF_CHUNK, :])
        carry_ref[...] = y

    _deferred_norm_steps(carry_ref, o_ref, g2_ref, b2_ref, NORM_BLOCKS_FFN, tile_body)


def _resident(shape):
    nd = len(shape)
    return pl.BlockSpec(shape, lambda i: (0,) * nd, pipeline_mode=pl.Buffered(1))


def _mix_layer(x2d, w_in, b_gate, conv_w, vg, vb, w_s, bs_full, w_pa, w_pb, w_o, g1, b1,
               w_ff1, w_ff2, *, alpha):
    s, d = x2d.shape
    w_a = w_pa.shape[0]
    w_b = w_pb.shape[0]
    d_ff = w_ff1.shape[1]
    ts = SEQ_TILE_MIX
    assert s % ts == 0 and ts % CHUNK == 0
    n_tiles = s // ts
    hb = ts // HALO
    n_halo_blocks = s // HALO
    assert ts % HALO == 0
    r1, r2 = d // n_tiles, d_ff // n_tiles
    assert r1 * n_tiles == d and r2 * n_tiles == d_ff
    assert r1 % BF16_ROWS == 0 and r2 % BF16_ROWS == 0
    last = n_tiles - 1
    tile = lambda i: jnp.minimum(i, last)
    kern = functools.partial(_mix_kernel, alpha=alpha, w_a=w_a, w_b=w_b, d_model=d)
    in_hbm = pl.BlockSpec(memory_space=pl.ANY)
    return pl.pallas_call(
        kern,
        grid=(n_tiles + 1,),
        in_specs=[
            pl.BlockSpec((ts, d), lambda i: (tile(i), 0)),
            pl.BlockSpec((HALO, d),
                         lambda i: (jnp.minimum((tile(i) + 1) * hb, n_halo_blocks - 1), 0)),
            in_hbm, _resident(b_gate.shape), _resident(conv_w.shape),
            _resident(vg.shape), _resident(vb.shape), _resident(w_s.shape),
            _resident(bs_full.shape), in_hbm, in_hbm, in_hbm,
            _resident(g1.shape), _resident(b1.shape),
            pl.BlockSpec((r1, d_ff), lambda i: (tile(i), 0)),
            pl.BlockSpec((r2, d), lambda i: (tile(i), 0)),
        ],
        out_specs=[
            pl.BlockSpec((ts, d), lambda i: (jnp.maximum(i - 1, 0), 0)),
            pl.BlockSpec((r1, d_ff), lambda i: (tile(i), 0)),
            pl.BlockSpec((r2, d), lambda i: (tile(i), 0)),
        ],
        out_shape=[
            jax.ShapeDtypeStruct((s, d), jnp.float32),
            jax.ShapeDtypeStruct(w_ff1.shape, jnp.bfloat16),
            jax.ShapeDtypeStruct(w_ff2.shape, jnp.bfloat16),
        ],
        scratch_shapes=[
            pltpu.VMEM(w_in.shape, jnp.bfloat16),
            pltpu.VMEM(w_pa.shape, jnp.bfloat16),
            pltpu.VMEM(w_pb.shape, jnp.bfloat16),
            pltpu.VMEM(w_o.shape, jnp.bfloat16),
            pltpu.VMEM((ts + HALO, d), jnp.bfloat16),
            pltpu.VMEM((ts + 2 * HALO, w_a), jnp.float32),
            pltpu.VMEM((ts, d), jnp.float32),
        ],
        compiler_params=pltpu.CompilerParams(
            dimension_semantics=("arbitrary",), vmem_limit_bytes=VMEM_LIMIT_BYTES),
        name="token_mix",
    )(x2d, x2d, w_in, b_gate, conv_w, vg, vb, w_s, bs_full, w_pa, w_pb, w_o, g1, b1,
      w_ff1, w_ff2)


def _ffn_layer(x2d, w1, w2, g2, b2, *, alpha):
    s, d = x2d.shape
    ts = SEQ_TILE_FFN
    assert s % ts == 0
    n_tiles = s // ts
    return pl.pallas_call(
        functools.partial(_ffn_kernel, alpha=alpha),
        grid=(n_tiles + 1,),
        in_specs=[
            pl.BlockSpec((ts, d), lambda i: (jnp.minimum(i, n_tiles - 1), 0)),
            _resident(w1.shape), _resident(w2.shape), _resident(g2.shape), _resident(b2.shape),
        ],
        out_specs=pl.BlockSpec((ts, d), lambda i: (jnp.maximum(i - 1, 0), 0)),
        out_shape=jax.ShapeDtypeStruct((s, d), jnp.float32),
        scratch_shapes=[pltpu.VMEM((ts, d), jnp.float32)],
        compiler_params=pltpu.CompilerParams(
            dimension_semantics=("arbitrary",), vmem_limit_bytes=VMEM_LIMIT_BYTES),
        name="channel_mix",
    )(x2d, w1, w2, g2, b2)


def kernel(x, w_in, b_gate, conv_w, v_norm_g, v_norm_b, w_s, b_s, w_pa, w_pb, w_o,
           ln1_g, ln1_b, w_ff1, w_ff2, ln2_g, ln2_b):
    bsz, seq, d = x.shape
    depth = w_in.shape[0]
    alpha = (2.0 * depth) ** 0.25
    bf = jnp.bfloat16
    row = lambda p: p.reshape(1, -1)
    outs = []
    for b in range(bsz):
        xb = x[b]
        for l in range(depth):
            bs_full = jnp.repeat(jnp.transpose(b_s[l]), DH_B, axis=1)
            xb, w1b, w2b = _mix_layer(
                xb, w_in[l], row(b_gate[l]), conv_w[l], row(v_norm_g[l]),
                row(v_norm_b[l]), w_s[l].astype(bf), bs_full, w_pa[l], w_pb[l], w_o[l],
                row(ln1_g[l]), row(ln1_b[l]),
                w_ff1[l], w_ff2[l], alpha=alpha)
            xb = _ffn_layer(xb, w1b, w2b, row(ln2_g[l]), row(ln2_b[l]), alpha=alpha)
        outs.append(xb)
    return jnp.stack(outs, axis=0) if bsz > 1 else outs[0][None]
```

```python
import functools
import math

import jax
import jax.numpy as jnp
from jax import lax
from jax.experimental import pallas as pl
from jax.experimental.pallas import tpu as pltpu

CONV_K = 3
CHUNK = 128
DH_B = 128
LN_EPS = 1e-5
GELU_C1 = math.sqrt(2.0 / math.pi)
GELU_C2 = GELU_C1 * 0.044715

LANES = 128
BF16_ROWS = 16
HALO = BF16_ROWS
VMEM_LIMIT_BYTES = 60 * 1024 * 1024

SEQ_TILE_MIX = 512
SEQ_TILE_FFN = 512
FF_CHUNK = 1024
CH_PARTS = 3
V_PARTS = 4
NORM_BLOCKS_MIX = 2 * CH_PARTS + V_PARTS - 2
NORM_BLOCKS_FFN = 8
W_CHUNK = 512
STAGE_SLOTS = 4


def _layernorm(y, g, b, eps=LN_EPS):
    mu = jnp.mean(y, axis=-1, keepdims=True)
    yc = y - mu
    var = jnp.mean(yc * yc, axis=-1, keepdims=True)
    return yc * lax.rsqrt(var + eps) * g + b


def _twice_sigmoid(half_z):
    return 1.0 + jnp.tanh(half_z)


def _twice_gelu(x):
    t = jnp.tanh(x * (GELU_C1 + GELU_C2 * (x * x)))
    return x + x * t


def _dot(a, b):
    return jnp.dot(a, b, preferred_element_type=jnp.float32)


def _token(y):
    rows, cols = y.shape
    assert rows % BF16_ROWS == 0 and cols % LANES == 0
    t = y.reshape(rows // BF16_ROWS, BF16_ROWS, cols).sum(axis=0)
    t = functools.reduce(
        lambda p, q: p + q, [t[:, c * LANES:(c + 1) * LANES] for c in range(cols // LANES)])
    return t.astype(jnp.bfloat16)


def _norm_carried_tile(carry_ref, o_ref, g_ref, b_ref, n_blocks):
    rows = carry_ref.shape[0] // n_blocks
    assert rows * n_blocks == carry_ref.shape[0]
    tokens = []
    for r in range(n_blocks):
        y = _layernorm(carry_ref[pl.ds(r * rows, rows), :], g_ref[...], b_ref[...])
        o_ref[pl.ds(r * rows, rows), :] = y
        tokens.append(_token(y))
    return tokens


def _tie(value, token):
    if token is None:
        return value
    never = pl.program_id(0) < 0
    r, c = token.shape
    head = jnp.where(never, token, value[0:r, 0:c])
    top = jnp.concatenate([head, value[0:r, c:]], axis=1)
    return jnp.concatenate([top, value[r:]], axis=0)


class _WeightStream:
    def __init__(self, chunks, stage, sem):
        self._scale = [scale for _, _, scale in chunks]
        self._dst = [dst for _, dst, _ in chunks]
        self._slot = [stage.at[k % STAGE_SLOTS] for k in range(len(chunks))]
        self._copy = [
            pltpu.make_async_copy(src, self._slot[k], sem.at[k % STAGE_SLOTS])
            for k, (src, _, _) in enumerate(chunks)]
        self._done = 0
        for k in range(min(STAGE_SLOTS - 1, len(chunks))):
            self._copy[k].start(priority=k % 2)

    def upto(self, n):
        ahead = STAGE_SLOTS - 1
        for k in range(self._done, n):
            if k + ahead < len(self._copy):
                self._copy[k + ahead].start(priority=(k + ahead) % 2)
            self._copy[k].wait()
            block = self._slot[k][...]
            if self._scale[k] != 1.0:
                block = block * self._scale[k]
            self._dst[k][...] = block.astype(jnp.bfloat16)
        self._done = max(self._done, n)

    def finish(self):
        self.upto(len(self._copy))


def _mix_kernel(x_ref, xnext_ref, w_in_hbm, bgate_ref, convw_ref,
                vg_ref, vb_ref, ws_ref, bs_ref, wpa_hbm, wpb_hbm, wo_hbm,
                g1_ref, b1_ref, w1f_ref, w2f_ref, o_ref, w1b_ref, w2b_ref,
                w_in_ref, wpa_ref, wpb_ref, wo_ref, xe_ref, ch_ref, carry_ref,
                *, alpha, w_a, w_b, d_model):
    ts = x_ref.shape[0]
    i = pl.program_id(0)
    n_tiles = pl.num_programs(0) - 1
    off_ca = w_a
    off_ha = 2 * w_a
    off_ub = 3 * w_a
    off_vb = off_ub + w_b
    off_ga = off_vb + w_b
    off_gb = off_ga + d_model
    n_proj = off_gb + d_model
    ch_w = w_a // CH_PARTS
    v_w = w_b // V_PARTS

    ch_order = [q for p in range(CH_PARTS)
                for q in ((f"c{p}", "w_in", off_ca + p * ch_w, ch_w),
                          (f"h{p}", "w_in", off_ha + p * ch_w, ch_w))]
    v_order = [(f"v{q}", "w_in", off_vb + q * v_w, v_w) for q in range(V_PARTS)]
    first_phase = []
    for j, entry in enumerate(ch_order):
        first_phase.append(entry)
        if j < V_PARTS:
            first_phase.append(v_order[j])
    matmuls = first_phase + [(f"b{p}", "w_in", p * ch_w, ch_w) for p in range(CH_PARTS)] + [
        ("u", "w_in", off_ub, w_b), ("gate_b", "w_in", off_gb, d_model),
        ("w_pb", "w_pb", 0, d_model), ("gate_a", "w_in", off_ga, d_model),
        ("w_pa", "w_pa", 0, d_model), ("w_o", "w_o", 0, d_model)]
    weights = {"w_in": (w_in_hbm, w_in_ref), "w_pa": (wpa_hbm, wpa_ref),
               "w_pb": (wpb_hbm, wpb_ref), "w_o": (wo_hbm, wo_ref)}

    def weight_scale(weight, col):
        if weight == "w_in":
            return 0.5 if col >= off_ga else 1.0
        return {"w_pa": 1.0, "w_pb": 0.5, "w_o": 0.5}[weight]

    def stream_plan():
        chunks, needed, staged = [], {}, set()
        for name, weight, col0, width in matmuls:
            src, dst = weights[weight]
            assert src.shape[0] % W_CHUNK == 0 and src.shape[1] % W_CHUNK == 0
            for cb in range(col0 // W_CHUNK, (col0 + width - 1) // W_CHUNK + 1):
                if (weight, cb) in staged:
                    continue
                staged.add((weight, cb))
                for r0 in range(0, src.shape[0], W_CHUNK):
                    view = (pl.ds(r0, W_CHUNK), pl.ds(cb * W_CHUNK, W_CHUNK))
                    chunks.append((src.at[view], dst.at[view], weight_scale(weight, cb * W_CHUNK)))
            needed[name] = len(chunks)
        total = sum(src.shape[0] * src.shape[1] for src, _ in weights.values())
        assert len(chunks) * W_CHUNK * W_CHUNK == total
        return chunks, needed

    def tile_body(tokens, ready):
        w1b_ref[...] = w1f_ref[...].astype(jnp.bfloat16)
        w2b_ref[...] = w2f_ref[...].astype(jnp.bfloat16)

        x = x_ref[...]
        xe_ref[pl.ds(0, ts), :] = x.astype(jnp.bfloat16)
        xe_ref[pl.ds(ts, HALO), :] = jnp.where(
            i < n_tiles - 1, xnext_ref[...], 0.0).astype(jnp.bfloat16)
        xe = xe_ref[...]
        xm = xe_ref[pl.ds(0, ts), :]
        ch_ref[pl.ds(0, HALO), :] = ch_ref[pl.ds(ts, HALO), :]
        n_chunks = ts // CHUNK
        n_heads = w_b // DH_B

        def project(lhs, name):
            (_, weight, col0, width), = [m for m in matmuls if m[0] == name]
            assert weight == "w_in"
            ready(name)
            return _dot(lhs, w_in_ref[:, col0:col0 + width])

        tok = iter(tokens) if tokens is not None else None
        results = {}
        for j, (name, _, _, _) in enumerate(first_phase):
            lhs = xm if name.startswith("v") else xe
            if tok is not None and j >= 2:
                lhs = _tie(lhs, next(tok))
            results[name] = project(lhs, name)
            if name.startswith("v"):
                results[name] = _twice_gelu(results[name])
        assert tok is None or next(tok, None) is None
        for p in range(CH_PARTS):
            ch_ref[pl.ds(HALO, ts + HALO), p * ch_w:(p + 1) * ch_w] = (
                results[f"c{p}"] * results[f"h{p}"])
        v = _layernorm(jnp.concatenate([results[f"v{q}"] for q in range(V_PARTS)], axis=1),
                       vg_ref[...], vb_ref[...], eps=4.0 * LN_EPS)
        v = v.astype(jnp.bfloat16)

        cw = convw_ref[...]
        a_parts = []
        for p in range(CH_PARTS):
            cols = slice(p * ch_w, (p + 1) * ch_w)
            conv = (ch_ref[pl.ds(HALO - 1, ts), cols] * cw[0:1, cols]
                    + ch_ref[pl.ds(HALO, ts), cols] * cw[1:2, cols]
                    + ch_ref[pl.ds(HALO + 1, ts), cols] * cw[2:3, cols])
            a_parts.append((project(xm, f"b{p}") * conv).astype(jnp.bfloat16))
        a = jnp.concatenate(a_parts, axis=1)
        u2 = _twice_gelu(project(xm, "u"))

        head_cols = []
        for h in range(n_heads):
            vh = jnp.concatenate(
                [v[k * CHUNK:(k + 1) * CHUNK, h * DH_B:(h + 1) * DH_B] for k in range(n_chunks)],
                axis=1)
            mh = _dot(ws_ref[h], vh)
            mh = mh + jnp.concatenate([bs_ref[:, h * DH_B:(h + 1) * DH_B]] * n_chunks, axis=1)
            head_cols.append(jnp.concatenate(
                [mh[:, k * DH_B:(k + 1) * DH_B] for k in range(n_chunks)], axis=0))
        mixed = jnp.concatenate(head_cols, axis=1)

        half_bgate = 0.5 * bgate_ref[...]
        g2_b = _twice_sigmoid(project(xm, "gate_b") + half_bgate[:, d_model:])
        ready("w_pb")
        y_b = _dot((u2 * mixed).astype(jnp.bfloat16), wpb_ref[...])
        g2_a = _twice_sigmoid(project(xm, "gate_a") + half_bgate[:, :d_model])
        acc2 = g2_b * y_b
        ready("w_pa")
        y_a = _dot(a, wpa_ref[...])
        acc2 = acc2 + g2_a * y_a
        ready("w_o")
        carry_ref[...] = alpha * x + _dot(acc2.astype(jnp.bfloat16), wo_ref[...])

    @pl.when(i == 0)
    def _():
        ch_ref[pl.ds(ts, HALO), :] = jnp.zeros((HALO, ch_ref.shape[1]), jnp.float32)

        def first_tile(stage, sem):
            chunks, needed = stream_plan()
            stream = _WeightStream(chunks, stage, sem)
            tile_body(None, lambda name: stream.upto(needed[name]))
            stream.finish()

        pl.run_scoped(first_tile, pltpu.VMEM((STAGE_SLOTS, W_CHUNK, W_CHUNK), jnp.float32),
                      pltpu.SemaphoreType.DMA((STAGE_SLOTS,)))

    @pl.when(jnp.logical_and(i > 0, i < n_tiles))
    def _():
        tile_body(_norm_carried_tile(carry_ref, o_ref, g1_ref, b1_ref, NORM_BLOCKS_MIX),
                  lambda name: None)

    @pl.when(i == n_tiles)
    def _():
        _norm_carried_tile(carry_ref, o_ref, g1_ref, b1_ref, NORM_BLOCKS_MIX)


def _ffn_kernel(x_ref, w1_ref, w2_ref, g2_ref, b2_ref, o_ref, carry_ref, *, alpha):
    n_ff = w1_ref.shape[1] // FF_CHUNK
    assert NORM_BLOCKS_FFN == 2 * n_ff
    i = pl.program_id(0)
    n_tiles = pl.num_programs(0) - 1

    def tile_body(tokens):
        x = x_ref[...]
        xb = x.astype(jnp.bfloat16)
        y = alpha * x
        for k in range(n_ff):
            lhs = xb if k == 0 else _tie(xb, tokens[2 * k - 1])
            hid = jnp.maximum(_dot(lhs, w1_ref[:, k * FF_CHUNK:(k + 1) * FF_CHUNK]), 0.0)
            hid = _tie((hid * hid).astype(jnp.bfloat16), tokens[2 * k])
            if k == n_ff - 1:
                hid = _tie(hid, tokens[2 * k + 1])
            y = y + _dot(hid, w2_ref[k * FF_CHUNK:(k + 1) * FF_CHUNK, :])
        carry_ref[...] = y

    @pl.when(i == 0)
    def _():
        carry_ref[...] = jnp.zeros_like(carry_ref)

    @pl.when(i < n_tiles)
    def _():
        tile_body(_norm_carried_tile(carry_ref, o_ref, g2_ref, b2_ref, NORM_BLOCKS_FFN))

    @pl.when(i == n_tiles)
    def _():
        _norm_carried_tile(carry_ref, o_ref, g2_ref, b2_ref, NORM_BLOCKS_FFN)


def _resident(shape):
    nd = len(shape)
    return pl.BlockSpec(shape, lambda i: (0,) * nd, pipeline_mode=pl.Buffered(1))


def _mix_layer(x2d, w_in, b_gate, conv_w, vg, vb, w_s, bs_full, w_pa, w_pb, w_o, g1, b1,
               w_ff1, w_ff2, *, alpha):
    s, d = x2d.shape
    w_a = w_pa.shape[0]
    w_b = w_pb.shape[0]
    d_ff = w_ff1.shape[1]
    ts = SEQ_TILE_MIX
    assert s % ts == 0 and ts % CHUNK == 0 and ts % HALO == 0
    n_tiles = s // ts
    hb = ts // HALO
    n_halo_blocks = s // HALO
    r1, r2 = d // n_tiles, d_ff // n_tiles
    assert r1 * n_tiles == d and r2 * n_tiles == d_ff
    assert r1 % BF16_ROWS == 0 and r2 % BF16_ROWS == 0
    last = n_tiles - 1
    tile = lambda i: jnp.minimum(i, last)
    kern = functools.partial(_mix_kernel, alpha=alpha, w_a=w_a, w_b=w_b, d_model=d)
    in_hbm = pl.BlockSpec(memory_space=pl.ANY)
    return pl.pallas_call(
        kern,
        grid=(n_tiles + 1,),
        in_specs=[
            pl.BlockSpec((ts, d), lambda i: (tile(i), 0)),
            pl.BlockSpec((HALO, d),
                         lambda i: (jnp.minimum((tile(i) + 1) * hb, n_halo_blocks - 1), 0)),
            in_hbm, _resident(b_gate.shape), _resident(conv_w.shape),
            _resident(vg.shape), _resident(vb.shape), _resident(w_s.shape),
            _resident(bs_full.shape), in_hbm, in_hbm, in_hbm,
            _resident(g1.shape), _resident(b1.shape),
            pl.BlockSpec((r1, d_ff), lambda i: (tile(i), 0)),
            pl.BlockSpec((r2, d), lambda i: (tile(i), 0)),
        ],
        out_specs=[
            pl.BlockSpec((ts, d), lambda i: (jnp.maximum(i - 1, 0), 0)),
            pl.BlockSpec((r1, d_ff), lambda i: (tile(i), 0)),
            pl.BlockSpec((r2, d), lambda i: (tile(i), 0)),
        ],
        out_shape=[
            jax.ShapeDtypeStruct((s, d), jnp.float32),
            jax.ShapeDtypeStruct(w_ff1.shape, jnp.bfloat16),
            jax.ShapeDtypeStruct(w_ff2.shape, jnp.bfloat16),
        ],
        scratch_shapes=[
            pltpu.VMEM(w_in.shape, jnp.bfloat16),
            pltpu.VMEM(w_pa.shape, jnp.bfloat16),
            pltpu.VMEM(w_pb.shape, jnp.bfloat16),
            pltpu.VMEM(w_o.shape, jnp.bfloat16),
            pltpu.VMEM((ts + HALO, d), jnp.bfloat16),
            pltpu.VMEM((ts + 2 * HALO, w_a), jnp.float32),
            pltpu.VMEM((ts, d), jnp.float32),
        ],
        compiler_params=pltpu.CompilerParams(
            dimension_semantics=("arbitrary",), vmem_limit_bytes=VMEM_LIMIT_BYTES),
        name="token_mix",
    )(x2d, x2d, w_in, b_gate, conv_w, vg, vb, w_s, bs_full, w_pa, w_pb, w_o, g1, b1,
      w_ff1, w_ff2)


def _ffn_layer(x2d, w1, w2, g2, b2, *, alpha):
    s, d = x2d.shape
    ts = SEQ_TILE_FFN
    assert s % ts == 0
    n_tiles = s // ts
    return pl.pallas_call(
        functools.partial(_ffn_kernel, alpha=alpha),
        grid=(n_tiles + 1,),
        in_specs=[
            pl.BlockSpec((ts, d), lambda i: (jnp.minimum(i, n_tiles - 1), 0)),
            _resident(w1.shape), _resident(w2.shape), _resident(g2.shape), _resident(b2.shape),
        ],
        out_specs=pl.BlockSpec((ts, d), lambda i: (jnp.maximum(i - 1, 0), 0)),
        out_shape=jax.ShapeDtypeStruct((s, d), jnp.float32),
        scratch_shapes=[pltpu.VMEM((ts, d), jnp.float32)],
        compiler_params=pltpu.CompilerParams(
            dimension_semantics=("arbitrary",), vmem_limit_bytes=VMEM_LIMIT_BYTES),
        name="channel_mix",
    )(x2d, w1, w2, g2, b2)


def kernel(x, w_in, b_gate, conv_w, v_norm_g, v_norm_b, w_s, b_s, w_pa, w_pb, w_o,
           ln1_g, ln1_b, w_ff1, w_ff2, ln2_g, ln2_b):
    bsz, seq, d = x.shape
    depth = w_in.shape[0]
    alpha = (2.0 * depth) ** 0.25
    bf = jnp.bfloat16
    row = lambda p: p.reshape(1, -1)
    outs = []
    for b in range(bsz):
        xb = x[b]
        for l in range(depth):
            bs_full = jnp.repeat(jnp.transpose(b_s[l]), DH_B, axis=1)
            xb, w1b, w2b = _mix_layer(
                xb, w_in[l], row(b_gate[l]), conv_w[l], row(v_norm_g[l]),
                row(v_norm_b[l]), w_s[l].astype(bf), bs_full, w_pa[l], w_pb[l], w_o[l],
                row(ln1_g[l]), row(ln1_b[l]),
                w_ff1[l], w_ff2[l], alpha=alpha)
            xb = _ffn_layer(xb, w1b, w2b, row(ln2_g[l]), row(ln2_b[l]), alpha=alpha)
        outs.append(xb)
    return jnp.stack(outs, axis=0) if bsz > 1 else outs[0][None]
```

```python
import functools
import math

import jax
import jax.numpy as jnp
from jax import lax
from jax.experimental import pallas as pl
from jax.experimental.pallas import tpu as pltpu

CONV_K = 3
CHUNK = 128
DH_B = 128
LN_EPS = 1e-5
GELU_C1 = math.sqrt(2.0 / math.pi)
GELU_C2 = GELU_C1 * 0.044715

LANES = 128
BF16_ROWS = 16
HALO = BF16_ROWS
VMEM_LIMIT_BYTES = 60 * 1024 * 1024

SEQ_TILE_MIX = 512
SEQ_TILE_FFN = 512
FF_CHUNK = 1024
CH_PARTS = 3
V_PARTS = 4
NORM_BLOCKS_MIX = 2 * CH_PARTS + V_PARTS - 2
NORM_BLOCKS_FFN = 8
W_CHUNK = 512
STAGE_SLOTS = 6


def _layernorm(y, g, b, eps=LN_EPS):
    mu = jnp.mean(y, axis=-1, keepdims=True)
    yc = y - mu
    var = jnp.mean(yc * yc, axis=-1, keepdims=True)
    return yc * lax.rsqrt(var + eps) * g + b


def _twice_sigmoid(half_z):
    return 1.0 + jnp.tanh(half_z)


def _twice_gelu(x):
    t = jnp.tanh(x * (GELU_C1 + GELU_C2 * (x * x)))
    return x + x * t


def _dot(a, b):
    return jnp.dot(a, b, preferred_element_type=jnp.float32)


def _token(y):
    rows, cols = y.shape
    assert rows % BF16_ROWS == 0 and cols % LANES == 0
    t = y.reshape(rows // BF16_ROWS, BF16_ROWS, cols).sum(axis=0)
    t = functools.reduce(
        lambda p, q: p + q, [t[:, c * LANES:(c + 1) * LANES] for c in range(cols // LANES)])
    return t.astype(jnp.bfloat16)


def _norm_carried_tile(carry_ref, o_ref, g_ref, b_ref, n_blocks):
    rows = carry_ref.shape[0] // n_blocks
    assert rows * n_blocks == carry_ref.shape[0]
    tokens = []
    for r in range(n_blocks):
        y = _layernorm(carry_ref[pl.ds(r * rows, rows), :], g_ref[...], b_ref[...])
        o_ref[pl.ds(r * rows, rows), :] = y
        tokens.append(_token(y))
    return tokens


def _tie(value, token):
    if token is None:
        return value
    never = pl.program_id(0) < 0
    r, c = token.shape
    head = jnp.where(never, token, value[0:r, 0:c])
    top = jnp.concatenate([head, value[0:r, c:]], axis=1)
    return jnp.concatenate([top, value[r:]], axis=0)


class _WeightStream:
    def __init__(self, chunks, stage, sem):
        self._scale = [scale for _, _, scale in chunks]
        self._dst = [dst for _, dst, _ in chunks]
        self._slot = [stage.at[k % STAGE_SLOTS] for k in range(len(chunks))]
        self._copy = [
            pltpu.make_async_copy(src, self._slot[k], sem.at[k % STAGE_SLOTS])
            for k, (src, _, _) in enumerate(chunks)]
        self._done = 0
        for k in range(min(STAGE_SLOTS - 1, len(chunks))):
            self._copy[k].start(priority=k % 2)

    def upto(self, n):
        ahead = STAGE_SLOTS - 1
        for k in range(self._done, n):
            if k + ahead < len(self._copy):
                self._copy[k + ahead].start(priority=(k + ahead) % 2)
            self._copy[k].wait()
            block = self._slot[k][...]
            if self._scale[k] != 1.0:
                block = block * self._scale[k]
            self._dst[k][...] = block.astype(jnp.bfloat16)
        self._done = max(self._done, n)

    def finish(self):
        self.upto(len(self._copy))


def _mix_kernel(x_ref, xnext_ref, w_in_hbm, bgate_ref, convw_ref,
                vg_ref, vb_ref, ws_ref, bs_ref, wpa_hbm, wpb_hbm, wo_hbm,
                g1_ref, b1_ref, w1f_ref, w2f_ref, o_ref, w1b_ref, w2b_ref,
                w_in_ref, wpa_ref, wpb_ref, wo_ref, xe_ref, ch_ref, carry_ref,
                *, alpha, w_a, w_b, d_model):
    ts = x_ref.shape[0]
    i = pl.program_id(0)
    n_tiles = pl.num_programs(0) - 1
    off_ca = w_a
    off_ha = 2 * w_a
    off_ub = 3 * w_a
    off_vb = off_ub + w_b
    off_ga = off_vb + w_b
    off_gb = off_ga + d_model
    n_proj = off_gb + d_model
    ch_w = w_a // CH_PARTS
    v_w = w_b // V_PARTS

    ch_order = [q for p in range(CH_PARTS)
                for q in ((f"c{p}", "w_in", off_ca + p * ch_w, ch_w),
                          (f"h{p}", "w_in", off_ha + p * ch_w, ch_w))]
    v_order = [(f"v{q}", "w_in", off_vb + q * v_w, v_w) for q in range(V_PARTS)]
    first_phase = []
    for j, entry in enumerate(ch_order):
        first_phase.append(entry)
        if j < V_PARTS:
            first_phase.append(v_order[j])
    matmuls = first_phase + [(f"b{p}", "w_in", p * ch_w, ch_w) for p in range(CH_PARTS)] + [
        ("u", "w_in", off_ub, w_b), ("gate_b", "w_in", off_gb, d_model),
        ("w_pb", "w_pb", 0, d_model), ("gate_a", "w_in", off_ga, d_model),
        ("w_pa", "w_pa", 0, d_model), ("w_o", "w_o", 0, d_model)]
    weights = {"w_in": (w_in_hbm, w_in_ref), "w_pa": (wpa_hbm, wpa_ref),
               "w_pb": (wpb_hbm, wpb_ref), "w_o": (wo_hbm, wo_ref)}

    def weight_scale(weight, col):
        if weight == "w_in":
            return 0.5 if col >= off_ga else 1.0
        return {"w_pa": 1.0, "w_pb": 0.5, "w_o": 0.5}[weight]

    def stream_plan():
        chunks, needed, staged = [], {}, set()
        for name, weight, col0, width in matmuls:
            src, dst = weights[weight]
            assert src.shape[0] % W_CHUNK == 0 and src.shape[1] % W_CHUNK == 0
            for cb in range(col0 // W_CHUNK, (col0 + width - 1) // W_CHUNK + 1):
                if (weight, cb) in staged:
                    continue
                staged.add((weight, cb))
                for r0 in range(0, src.shape[0], W_CHUNK):
                    view = (pl.ds(r0, W_CHUNK), pl.ds(cb * W_CHUNK, W_CHUNK))
                    chunks.append((src.at[view], dst.at[view], weight_scale(weight, cb * W_CHUNK)))
            needed[name] = len(chunks)
        total = sum(src.shape[0] * src.shape[1] for src, _ in weights.values())
        assert len(chunks) * W_CHUNK * W_CHUNK == total
        return chunks, needed

    def tile_body(tokens, ready):
        w1b_ref[...] = w1f_ref[...].astype(jnp.bfloat16)
        w2b_ref[...] = w2f_ref[...].astype(jnp.bfloat16)

        x = x_ref[...]
        xe_ref[pl.ds(0, ts), :] = x.astype(jnp.bfloat16)
        xe_ref[pl.ds(ts, HALO), :] = jnp.where(
            i < n_tiles - 1, xnext_ref[...], 0.0).astype(jnp.bfloat16)
        xe = xe_ref[...]
        xm = xe_ref[pl.ds(0, ts), :]
        ch_ref[pl.ds(0, HALO), :] = ch_ref[pl.ds(ts, HALO), :]
        n_chunks = ts // CHUNK
        n_heads = w_b // DH_B

        def project(lhs, name):
            (_, weight, col0, width), = [m for m in matmuls if m[0] == name]
            assert weight == "w_in"
            ready(name)
            return _dot(lhs, w_in_ref[:, col0:col0 + width])

        tok = iter(tokens) if tokens is not None else None
        results = {}
        for j, (name, _, _, _) in enumerate(first_phase):
            lhs = xm if name.startswith("v") else xe
            if tok is not None and j >= 2:
                lhs = _tie(lhs, next(tok))
            results[name] = project(lhs, name)
            if name.startswith("v"):
                results[name] = _twice_gelu(results[name])
        assert tok is None or next(tok, None) is None
        for p in range(CH_PARTS):
            ch_ref[pl.ds(HALO, ts + HALO), p * ch_w:(p + 1) * ch_w] = (
                results[f"c{p}"] * results[f"h{p}"])
        v = _layernorm(jnp.concatenate([results[f"v{q}"] for q in range(V_PARTS)], axis=1),
                       vg_ref[...], vb_ref[...], eps=4.0 * LN_EPS)
        v = v.astype(jnp.bfloat16)

        cw = convw_ref[...]
        a_parts = []
        for p in range(CH_PARTS):
            cols = slice(p * ch_w, (p + 1) * ch_w)
            conv = (ch_ref[pl.ds(HALO - 1, ts), cols] * cw[0:1, cols]
                    + ch_ref[pl.ds(HALO, ts), cols] * cw[1:2, cols]
                    + ch_ref[pl.ds(HALO + 1, ts), cols] * cw[2:3, cols])
            a_parts.append((project(xm, f"b{p}") * conv).astype(jnp.bfloat16))
        a = jnp.concatenate(a_parts, axis=1)
        u2 = _twice_gelu(project(xm, "u"))

        head_cols = []
        for h in range(n_heads):
            vh = jnp.concatenate(
                [v[k * CHUNK:(k + 1) * CHUNK, h * DH_B:(h + 1) * DH_B] for k in range(n_chunks)],
                axis=1)
            mh = _dot(ws_ref[h], vh)
            mh = mh + jnp.concatenate([bs_ref[:, h * DH_B:(h + 1) * DH_B]] * n_chunks, axis=1)
            head_cols.append(jnp.concatenate(
                [mh[:, k * DH_B:(k + 1) * DH_B] for k in range(n_chunks)], axis=0))
        mixed = jnp.concatenate(head_cols, axis=1)

        half_bgate = 0.5 * bgate_ref[...]
        g2_b = _twice_sigmoid(project(xm, "gate_b") + half_bgate[:, d_model:])
        ready("w_pb")
        y_b = _dot((u2 * mixed).astype(jnp.bfloat16), wpb_ref[...])
        g2_a = _twice_sigmoid(project(xm, "gate_a") + half_bgate[:, :d_model])
        acc2 = g2_b * y_b
        ready("w_pa")
        y_a = _dot(a, wpa_ref[...])
        acc2 = acc2 + g2_a * y_a
        ready("w_o")
        carry_ref[...] = alpha * x + _dot(acc2.astype(jnp.bfloat16), wo_ref[...])

    @pl.when(i == 0)
    def _():
        ch_ref[pl.ds(ts, HALO), :] = jnp.zeros((HALO, ch_ref.shape[1]), jnp.float32)

        def first_tile(stage, sem):
            chunks, needed = stream_plan()
            stream = _WeightStream(chunks, stage, sem)
            tile_body(None, lambda name: stream.upto(needed[name]))
            stream.finish()

        pl.run_scoped(first_tile, pltpu.VMEM((STAGE_SLOTS, W_CHUNK, W_CHUNK), jnp.float32),
                      pltpu.SemaphoreType.DMA((STAGE_SLOTS,)))

    @pl.when(jnp.logical_and(i > 0, i < n_tiles))
    def _():
        tile_body(_norm_carried_tile(carry_ref, o_ref, g1_ref, b1_ref, NORM_BLOCKS_MIX),
                  lambda name: None)

    @pl.when(i == n_tiles)
    def _():
        _norm_carried_tile(carry_ref, o_ref, g1_ref, b1_ref, NORM_BLOCKS_MIX)


def _ffn_kernel(x_ref, w1_hbm, w2_hbm, g2_ref, b2_ref, o_ref, w1_ref, w2_ref, carry_ref, sem,
                *, alpha):
    n_ff = w1_ref.shape[1] // FF_CHUNK
    assert NORM_BLOCKS_FFN == 2 * n_ff
    i = pl.program_id(0)
    n_tiles = pl.num_programs(0) - 1

    def tile_body(tokens, ready):
        x = x_ref[...]
        xb = x.astype(jnp.bfloat16)
        y = alpha * x
        for k in range(n_ff):
            ready(k)
            tok = (lambda j: tokens[j]) if tokens is not None else (lambda j: None)
            lhs = xb if k == 0 else _tie(xb, tok(2 * k - 1))
            hid = jnp.maximum(_dot(lhs, w1_ref[:, k * FF_CHUNK:(k + 1) * FF_CHUNK]), 0.0)
            hid = _tie((hid * hid).astype(jnp.bfloat16), tok(2 * k))
            if k == n_ff - 1:
                hid = _tie(hid, tok(2 * k + 1))
            y = y + _dot(hid, w2_ref[k * FF_CHUNK:(k + 1) * FF_CHUNK, :])
        carry_ref[...] = y

    @pl.when(i == 0)
    def _():
        copies = []
        for k in range(n_ff):
            cols = pl.ds(k * FF_CHUNK, FF_CHUNK)
            copies.append((
                pltpu.make_async_copy(w1_hbm.at[:, cols], w1_ref.at[:, cols], sem.at[0, k]),
                pltpu.make_async_copy(w2_hbm.at[cols, :], w2_ref.at[cols, :], sem.at[1, k])))
        for k, pair in enumerate(copies):
            for copy in pair:
                copy.start(priority=k % 2)

        def landed(k):
            for copy in copies[k]:
                copy.wait()

        tile_body(None, landed)

    @pl.when(jnp.logical_and(i > 0, i < n_tiles))
    def _():
        tile_body(_norm_carried_tile(carry_ref, o_ref, g2_ref, b2_ref, NORM_BLOCKS_FFN),
                  lambda k: None)

    @pl.when(i == n_tiles)
    def _():
        _norm_carried_tile(carry_ref, o_ref, g2_ref, b2_ref, NORM_BLOCKS_FFN)


def _resident(shape):
    nd = len(shape)
    return pl.BlockSpec(shape, lambda i: (0,) * nd, pipeline_mode=pl.Buffered(1))


def _mix_layer(x2d, w_in, b_gate, conv_w, vg, vb, w_s, bs_full, w_pa, w_pb, w_o, g1, b1,
               w_ff1, w_ff2, *, alpha):
    s, d = x2d.shape
    w_a = w_pa.shape[0]
    w_b = w_pb.shape[0]
    d_ff = w_ff1.shape[1]
    ts = SEQ_TILE_MIX
    assert s % ts == 0 and ts % CHUNK == 0 and ts % HALO == 0
    n_tiles = s // ts
    hb = ts // HALO
    n_halo_blocks = s // HALO
    r1, r2 = d // n_tiles, d_ff // n_tiles
    assert r1 * n_tiles == d and r2 * n_tiles == d_ff
    assert r1 % BF16_ROWS == 0 and r2 % BF16_ROWS == 0
    last = n_tiles - 1
    tile = lambda i: jnp.minimum(i, last)
    kern = functools.partial(_mix_kernel, alpha=alpha, w_a=w_a, w_b=w_b, d_model=d)
    in_hbm = pl.BlockSpec(memory_space=pl.ANY)
    return pl.pallas_call(
        kern,
        grid=(n_tiles + 1,),
        in_specs=[
            pl.BlockSpec((ts, d), lambda i: (tile(i), 0)),
            pl.BlockSpec((HALO, d),
                         lambda i: (jnp.minimum((tile(i) + 1) * hb, n_halo_blocks - 1), 0)),
            in_hbm, _resident(b_gate.shape), _resident(conv_w.shape),
            _resident(vg.shape), _resident(vb.shape), _resident(w_s.shape),
            _resident(bs_full.shape), in_hbm, in_hbm, in_hbm,
            _resident(g1.shape), _resident(b1.shape),
            pl.BlockSpec((r1, d_ff), lambda i: (tile(i), 0)),
            pl.BlockSpec((r2, d), lambda i: (tile(i), 0)),
        ],
        out_specs=[
            pl.BlockSpec((ts, d), lambda i: (jnp.maximum(i - 1, 0), 0)),
            pl.BlockSpec((r1, d_ff), lambda i: (tile(i), 0)),
            pl.BlockSpec((r2, d), lambda i: (tile(i), 0)),
        ],
        out_shape=[
            jax.ShapeDtypeStruct((s, d), jnp.float32),
            jax.ShapeDtypeStruct(w_ff1.shape, jnp.bfloat16),
            jax.ShapeDtypeStruct(w_ff2.shape, jnp.bfloat16),
        ],
        scratch_shapes=[
            pltpu.VMEM(w_in.shape, jnp.bfloat16),
            pltpu.VMEM(w_pa.shape, jnp.bfloat16),
            pltpu.VMEM(w_pb.shape, jnp.bfloat16),
            pltpu.VMEM(w_o.shape, jnp.bfloat16),
            pltpu.VMEM((ts + HALO, d), jnp.bfloat16),
            pltpu.VMEM((ts + 2 * HALO, w_a), jnp.float32),
            pltpu.VMEM((ts, d), jnp.float32),
        ],
        compiler_params=pltpu.CompilerParams(
            dimension_semantics=("arbitrary",), vmem_limit_bytes=VMEM_LIMIT_BYTES),
        name="token_mix",
    )(x2d, x2d, w_in, b_gate, conv_w, vg, vb, w_s, bs_full, w_pa, w_pb, w_o, g1, b1,
      w_ff1, w_ff2)


def _ffn_layer(x2d, w1, w2, g2, b2, *, alpha):
    s, d = x2d.shape
    ts = SEQ_TILE_FFN
    assert s % ts == 0
    n_tiles = s // ts
    return pl.pallas_call(
        functools.partial(_ffn_kernel, alpha=alpha),
        grid=(n_tiles + 1,),
        in_specs=[
            pl.BlockSpec((ts, d), lambda i: (jnp.minimum(i, n_tiles - 1), 0)),
            pl.BlockSpec(memory_space=pl.ANY), pl.BlockSpec(memory_space=pl.ANY),
            _resident(g2.shape), _resident(b2.shape),
        ],
        out_specs=pl.BlockSpec((ts, d), lambda i: (jnp.maximum(i - 1, 0), 0)),
        out_shape=jax.ShapeDtypeStruct((s, d), jnp.float32),
        scratch_shapes=[
            pltpu.VMEM(w1.shape, w1.dtype),
            pltpu.VMEM(w2.shape, w2.dtype),
            pltpu.VMEM((ts, d), jnp.float32),
            pltpu.SemaphoreType.DMA((2, w1.shape[1] // FF_CHUNK)),
        ],
        compiler_params=pltpu.CompilerParams(
            dimension_semantics=("arbitrary",), vmem_limit_bytes=VMEM_LIMIT_BYTES),
        name="channel_mix",
    )(x2d, w1, w2, g2, b2)


def kernel(x, w_in, b_gate, conv_w, v_norm_g, v_norm_b, w_s, b_s, w_pa, w_pb, w_o,
           ln1_g, ln1_b, w_ff1, w_ff2, ln2_g, ln2_b):
    bsz, seq, d = x.shape
    depth = w_in.shape[0]
    alpha = (2.0 * depth) ** 0.25
    bf = jnp.bfloat16
    row = lambda p: p.reshape(1, -1)
    outs = []
    for b in range(bsz):
        xb = x[b]
        for l in range(depth):
            bs_full = jnp.repeat(jnp.transpose(b_s[l]), DH_B, axis=1)
            xb, w1b, w2b = _mix_layer(
                xb, w_in[l], row(b_gate[l]), conv_w[l], row(v_norm_g[l]),
                row(v_norm_b[l]), w_s[l].astype(bf), bs_full, w_pa[l], w_pb[l], w_o[l],
                row(ln1_g[l]), row(ln1_b[l]),
                w_ff1[l], w_ff2[l], alpha=alpha)
            xb = _ffn_layer(xb, w1b, w2b, row(ln2_g[l]), row(ln2_b[l]), alpha=alpha)
        outs.append(xb)
    return jnp.stack(outs, axis=0) if bsz > 1 else outs[0][None]
```
